```python
import jax, jax.numpy as jnp
from jax import lax
import numpy as np

D_MODEL = 4096
BATCH = 4
SEQ = 2048
DEPTH = 1
DEC_BATCH = 128
DEC_SEQ = 4
PAST_LEN = 16384
PAGE_SIZE = 128

N_MEM = 256
D_A = D_MODEL // 2
HEAD_A = 64
H_A = D_A // HEAD_A
LORA_W = 96
LORA_A = 96
LORA_G = 256
GN_EPS = 64e-5
D_B = D_MODEL // 4
CONV_W = 3
H_C = 4
D_C = D_MODEL // 4
HEAD_C = D_C // H_C
D_FF = 11008
EPS = 1e-6
P_A = 3 * D_A + LORA_W + LORA_A + LORA_G
P_B = 3 * D_B
P_C = D_C
P_G = 3 * D_MODEL
P_IN = P_A + P_B + P_C + P_G

kernel_name = "rwkv7_shortconv_memattn_macaron_step"


def rmsnorm(x, g):
    xf = x.astype(jnp.float32)
    y = xf * lax.rsqrt(jnp.mean(xf * xf, axis=-1, keepdims=True) + EPS)
    return (y * g.astype(jnp.float32)).astype(x.dtype)


def swiglu(x, w_in, w_out):
    gate, up = jnp.split(x @ w_in, 2, axis=-1)
    return (jax.nn.silu(gate) * up) @ w_out


def rwkv7_scan(s0, r, w, k, v, a_vec, b_vec):
    def step(s, inp):
        r_t, w_t, k_t, v_t, a_t, b_t = inp
        sa = jnp.einsum('bhij,bhj->bhi', s, a_t)
        s = s * w_t[:, :, None, :] + sa[..., None] * b_t[:, :, None, :] + v_t[..., None] * k_t[:, :, None, :]
        y = jnp.einsum('bhij,bhj->bhi', s, r_t)
        return s, y
    xs = tuple(jnp.moveaxis(t, 1, 0) for t in (r, w, k, v, a_vec, b_vec))
    s, ys = lax.scan(step, s0, xs)
    return jnp.moveaxis(ys, 0, 1), s


def rwkv7_branch(p_a, shift_prev, s_prev, lp):
    bsz, t_len, _ = p_a.shape
    pf = p_a.astype(jnp.float32)
    seq = jnp.concatenate([shift_prev[:, None].astype(jnp.float32), pf], axis=1)
    ps = pf + (seq[:, :-1] - pf) * lp['mu_shift'].astype(jnp.float32)
    new_shift = seq[:, -1]
    r, k, v, xw, xa, xg = jnp.split(ps, [D_A, 2 * D_A, 3 * D_A, 3 * D_A + LORA_W, 3 * D_A + LORA_W + LORA_A], axis=-1)
    w_log = -jax.nn.softplus(-(lp['w0'] + jnp.tanh(xw) @ lp['w_lora_up'])) - 0.5
    decay = jnp.exp(-jnp.exp(w_log))
    a = jax.nn.sigmoid(lp['a0'] + xa @ lp['a_lora_up'])
    g = jax.nn.sigmoid(xg) @ lp['g_lora_up']
    hs = lambda t: t.reshape(bsz, t_len, H_A, HEAD_A)
    kk = hs(k * lp['k_k'])
    kk = kk / jnp.maximum(jnp.sqrt(jnp.sum(kk * kk, axis=-1, keepdims=True)), 1e-12)
    k = k * (1.0 + (a - 1.0) * lp['k_a'])
    rh, kh, vh, ah = hs(r), hs(k), hs(v), hs(a)
    y, s_new = rwkv7_scan(s_prev.astype(jnp.float32), rh, hs(decay), kh, vh, -kk, kk * ah)
    mu = jnp.mean(y, axis=-1, keepdims=True)
    var = jnp.mean(jnp.square(y - mu), axis=-1, keepdims=True)
    y = ((y - mu) * lax.rsqrt(var + GN_EPS)).reshape(bsz, t_len, D_A) * lp['lnx_w'] + lp['lnx_b']
    y = y + (jnp.sum(rh * kh * lp['r_k'], axis=-1, keepdims=True) * vh).reshape(bsz, t_len, D_A)
    return y * g, s_new.astype(s_prev.dtype), new_shift.astype(shift_prev.dtype)


def shortconv_branch(p_b, buf, conv_w):
    gb, gc, u = jnp.split(p_b, 3, axis=-1)
    u = gc * u
    t_len = u.shape[1]
    seq = jnp.concatenate([buf.astype(u.dtype), u], axis=1)
    z = seq[:, 0:t_len] * conv_w[0]
    for j in range(1, CONV_W):
        z = z + seq[:, j:j + t_len] * conv_w[j]
    return gb * z, seq[:, -(CONV_W - 1):].astype(buf.dtype)


def memory_kv(mem, g_mem, w_mem_kv):
    bsz = mem.shape[0]
    mk, mv = jnp.split(rmsnorm(mem, g_mem) @ w_mem_kv, 2, axis=-1)
    return mk.reshape(bsz, N_MEM, H_C, HEAD_C), mv.reshape(bsz, N_MEM, H_C, HEAD_C)


def memory_attention(p_c, mem_k, mem_v):
    bsz, t_len, _ = p_c.shape
    q = p_c.reshape(bsz, t_len, H_C, HEAD_C)
    s = jnp.einsum('bthd,bmhd->bhtm', q, mem_k.astype(q.dtype)).astype(jnp.float32) * (HEAD_C ** -0.5)
    pr = jax.nn.softmax(s, axis=-1).astype(q.dtype)
    o = jnp.einsum('bhtm,bmhd->bthd', pr, mem_v.astype(q.dtype))
    return o.reshape(bsz, t_len, D_C)


def layer_forward(x, mem_k, mem_v, s_rwkv, s_shift, s_conv, lp):
    x = x + 0.5 * swiglu(rmsnorm(x, lp['g_ffn1']), lp['w_ffn1_in'], lp['w_ffn1_out'])
    h = rmsnorm(x, lp['g_mix'])
    p = h @ lp['w_in']
    p_a, p_b, p_c, p_g = jnp.split(p, [P_A, P_A + P_B, P_A + P_B + P_C], axis=-1)
    y_a, s_rwkv_new, s_shift_new = rwkv7_branch(p_a, s_shift, s_rwkv, lp)
    y_b, s_conv_new = shortconv_branch(p_b, s_conv, lp['conv_w'])
    y_c = memory_attention(p_c, mem_k, mem_v)
    g_a, g_b, g_c = jnp.split(jax.nn.sigmoid(p_g), 3, axis=-1)
    merged = (g_a * (y_a.astype(x.dtype) @ lp['w_br_a'])
              + g_b * (y_b @ lp['w_br_b'])
              + g_c * (y_c @ lp['w_br_c']))
    x = x + merged @ lp['w_out']
    x = x + 0.5 * swiglu(rmsnorm(x, lp['g_ffn2']), lp['w_ffn2_in'], lp['w_ffn2_out'])
    return x, s_rwkv_new, s_shift_new, s_conv_new


def setup_inputs(seed: int = 0) -> dict:
    key = jax.random.key(seed)
    ks = iter(jax.random.split(key, 64))
    L = DEPTH
    nrm = lambda shape, scale: jax.random.normal(next(ks), shape, jnp.float32) * scale
    uni = lambda shape, lo, hi: jax.random.uniform(next(ks), shape, jnp.float32, lo, hi)
    gain = lambda shape: 1.0 + nrm(shape, 0.02)
    return {
        'x_prompt': nrm((BATCH, SEQ, D_MODEL), 1.0),
        'x_sample': nrm((DEC_BATCH, DEC_SEQ, D_MODEL), 1.0),
        'mem_prompt': nrm((BATCH, N_MEM, D_MODEL), 1.0),
        'cache_mem_k': nrm((L, DEC_BATCH, N_MEM, H_C, HEAD_C), 1.0),
        'cache_mem_v': nrm((L, DEC_BATCH, N_MEM, H_C, HEAD_C), 1.0),
        'state_rwkv': nrm((L, DEC_BATCH, H_A, HEAD_A, HEAD_A), 1.0),
        'state_shift': nrm((L, DEC_BATCH, P_A), 1.0),
        'state_conv': nrm((L, DEC_BATCH, CONV_W - 1, D_B), 1.0),
        'g_ffn1': gain((L, D_MODEL)),
        'w_ffn1_in': nrm((L, D_MODEL, 2 * D_FF), D_MODEL ** -0.5),
        'w_ffn1_out': nrm((L, D_FF, D_MODEL), D_FF ** -0.5),
        'g_mix': gain((L, D_MODEL)),
        'w_in': nrm((L, D_MODEL, P_IN), D_MODEL ** -0.5),
        'mu_shift': uni((L, P_A), 0.0, 1.0),
        'w0': uni((L, D_A), -5.0, 1.0),
        'w_lora_up': nrm((L, LORA_W, D_A), 0.1),
        'a0': nrm((L, D_A), 0.1),
        'a_lora_up': nrm((L, LORA_A, D_A), 0.5 * LORA_A ** -0.5),
        'g_lora_up': nrm((L, LORA_G, D_A), LORA_G ** -0.5),
        'k_k': 0.85 + nrm((L, D_A), 0.02),
        'k_a': gain((L, D_A)),
        'r_k': nrm((L, H_A, HEAD_A), 0.1),
        'lnx_w': gain((L, D_A)),
        'lnx_b': nrm((L, D_A), 0.01),
        'conv_w': nrm((L, CONV_W, D_B), CONV_W ** -0.5),
        'g_mem': gain((L, D_MODEL)),
        'w_mem_kv': nrm((L, D_MODEL, 2 * D_C), D_MODEL ** -0.5),
        'w_br_a': nrm((L, D_A, D_MODEL), D_A ** -0.5),
        'w_br_b': nrm((L, D_B, D_MODEL), D_B ** -0.5),
        'w_br_c': nrm((L, D_C, D_MODEL), D_C ** -0.5),
        'w_out': nrm((L, D_MODEL, D_MODEL), D_MODEL ** -0.5),
        'g_ffn2': gain((L, D_MODEL)),
        'w_ffn2_in': nrm((L, D_MODEL, 2 * D_FF), D_MODEL ** -0.5),
        'w_ffn2_out': nrm((L, D_FF, D_MODEL), D_FF ** -0.5),
        'g_final': gain((D_MODEL,)),
    }


def reference(x_prompt, x_sample, mem_prompt, cache_mem_k, cache_mem_v, state_rwkv, state_shift, state_conv,
              g_ffn1, w_ffn1_in, w_ffn1_out, g_mix, w_in, mu_shift, w0, w_lora_up, a0, a_lora_up, g_lora_up,
              k_k, k_a, r_k, lnx_w, lnx_b, conv_w, g_mem, w_mem_kv, w_br_a, w_br_b, w_br_c, w_out,
              g_ffn2, w_ffn2_in, w_ffn2_out, g_final):
    xp, xs = x_prompt, x_sample
    bp = xp.shape[0]
    mk_list, mv_list = [], []
    rp_list, shp_list, cp_list = [], [], []
    rs_list, shs_list, cs_list = [], [], []
    for l in range(DEPTH):
        lp = {
            'g_ffn1': g_ffn1[l], 'w_ffn1_in': w_ffn1_in[l], 'w_ffn1_out': w_ffn1_out[l],
            'g_mix': g_mix[l], 'w_in': w_in[l], 'mu_shift': mu_shift[l],
            'w0': w0[l], 'w_lora_up': w_lora_up[l], 'a0': a0[l], 'a_lora_up': a_lora_up[l],
            'g_lora_up': g_lora_up[l], 'k_k': k_k[l], 'k_a': k_a[l], 'r_k': r_k[l],
            'lnx_w': lnx_w[l], 'lnx_b': lnx_b[l], 'conv_w': conv_w[l],
            'w_br_a': w_br_a[l], 'w_br_b': w_br_b[l], 'w_br_c': w_br_c[l], 'w_out': w_out[l],
            'g_ffn2': g_ffn2[l], 'w_ffn2_in': w_ffn2_in[l], 'w_ffn2_out': w_ffn2_out[l],
        }
        mk_p, mv_p = memory_kv(mem_prompt, g_mem[l], w_mem_kv[l])
        s0 = jnp.zeros((bp, H_A, HEAD_A, HEAD_A), xp.dtype)
        sh0 = jnp.zeros((bp, P_A), xp.dtype)
        cb0 = jnp.zeros((bp, CONV_W - 1, D_B), xp.dtype)
        xp, rp, shp, cp = layer_forward(xp, mk_p, mv_p, s0, sh0, cb0, lp)
        xs, rs, shs, cs = layer_forward(xs, cache_mem_k[l], cache_mem_v[l], state_rwkv[l], state_shift[l], state_conv[l], lp)
        mk_list.append(mk_p); mv_list.append(mv_p)
        rp_list.append(rp); shp_list.append(shp); cp_list.append(cp)
        rs_list.append(rs); shs_list.append(shs); cs_list.append(cs)
    y_prompt = rmsnorm(xp, g_final)
    y_sample = rmsnorm(xs, g_final)
    mem_k_prompt = jnp.stack(mk_list, 0)
    mem_v_prompt = jnp.stack(mv_list, 0)
    rwkv_prompt = jnp.stack(rp_list, 0)
    shift_prompt = jnp.stack(shp_list, 0)
    conv_prompt = jnp.stack(cp_list, 0)
    rwkv_sample = jnp.stack(rs_list, 0)
    shift_sample = jnp.stack(shs_list, 0)
    conv_sample = jnp.stack(cs_list, 0)
    return (y_prompt, y_sample, mem_k_prompt, mem_v_prompt, rwkv_prompt, shift_prompt, conv_prompt, rwkv_sample, shift_sample, conv_sample)
```

```python
import functools

import jax
import jax.numpy as jnp
from jax import lax
from jax.experimental import pallas as pl
from jax.experimental.pallas import tpu as pltpu

F32 = jnp.float32
BF16 = jnp.bfloat16

EPS = 1e-6
GN_EPS = 64e-5
HEAD_A = 64
CONV_W = 3
V7X_LANES = 128
V7X_SUBLANES = 8
V7X_VMEM_LIMIT = 56 * 1024 * 1024
W_LOG_SHIFT = 0.5
ATTN_KV_BLOCK_BYTES = 4 * 1024 * 1024


def _cp(sem, vmem=V7X_VMEM_LIMIT):
    return pltpu.CompilerParams(dimension_semantics=sem, vmem_limit_bytes=vmem)


def _tile(n, cap, mult=16):
    best = None
    for t in range(mult, min(n, cap) + 1, mult):
        if n % t == 0:
            best = t
    return best if best is not None else n


def _round_up(n, m):
    return (n + m - 1) // m * m


def _rms(x, g):
    ms = jnp.mean(x * x, axis=-1, keepdims=True)
    return (x * lax.rsqrt(ms + EPS)) * g


def _norm2_kernel(xp_ref, xs_ref, g_ref, x_ref, h_ref, *, n_prompt_tiles):
    i = pl.program_id(0)

    def emit(x):
        x_ref[...] = x
        h_ref[...] = _rms(x, g_ref[...]).astype(BF16)

    @pl.when(i < n_prompt_tiles)
    def _():
        emit(xp_ref[...])

    @pl.when(i >= n_prompt_tiles)
    def _():
        emit(xs_ref[...])


def _norm_concat(xp, xs, g):
    (n_p, d), n_s = xp.shape, xs.shape[0]
    tr = _tile(_gcd(n_p, n_s), 256, 8)
    npt, nst = n_p // tr, n_s // tr
    n = n_p + n_s
    return pl.pallas_call(
        functools.partial(_norm2_kernel, n_prompt_tiles=npt),
        grid=(npt + nst,),
        in_specs=[
            pl.BlockSpec((tr, d), lambda i: (jnp.minimum(i, npt - 1), 0)),
            pl.BlockSpec((tr, d), lambda i: (jnp.maximum(i - npt, 0), 0)),
            pl.BlockSpec((1, d), lambda i: (0, 0)),
        ],
        out_specs=[pl.BlockSpec((tr, d), lambda i: (i, 0)),
                   pl.BlockSpec((tr, d), lambda i: (i, 0))],
        out_shape=[jax.ShapeDtypeStruct((n, d), F32), jax.ShapeDtypeStruct((n, d), BF16)],
        compiler_params=_cp(("arbitrary",)),
        name="norm_concat",
    )(xp, xs, g.reshape(1, d))


def _gcd(a, b):
    while b:
        a, b = b, a % b
    return a


def _lcm(a, b):
    return a * b // _gcd(a, b)


def _add_norm_kernel(x_ref, d_ref, g_ref, xo_ref, h_ref):
    x = x_ref[...] + d_ref[...]
    xo_ref[...] = x
    h_ref[...] = _rms(x, g_ref[...]).astype(BF16)


def _add_norm(x, delta, g):
    n, d = x.shape
    tr = _tile(n, 256, 8)
    row = pl.BlockSpec((tr, d), lambda i: (i, 0))
    return pl.pallas_call(
        _add_norm_kernel,
        grid=(n // tr,),
        in_specs=[row, row, pl.BlockSpec((1, d), lambda i: (0, 0))],
        out_specs=[row, row],
        out_shape=[jax.ShapeDtypeStruct((n, d), F32), jax.ShapeDtypeStruct((n, d), BF16)],
        compiler_params=_cp(("arbitrary",)),
        name="add_norm",
    )(x, delta, g.reshape(1, d))


def _norm_kernel(x_ref, g_ref, h_ref):
    h_ref[...] = _rms(x_ref[...], g_ref[...]).astype(h_ref.dtype)


def _norm(x, g, out_dtype=BF16):
    n, d = x.shape
    tr = _tile(n, 256, 8)
    row = pl.BlockSpec((tr, d), lambda i: (i, 0))
    return pl.pallas_call(
        _norm_kernel,
        grid=(n // tr,),
        in_specs=[row, pl.BlockSpec((1, d), lambda i: (0, 0))],
        out_specs=row,
        out_shape=jax.ShapeDtypeStruct((n, d), out_dtype),
        compiler_params=_cp(("arbitrary",)),
        name="norm",
    )(x, g.reshape(1, d))


def _final_kernel(x_ref, d_ref, g_ref, y_ref):
    y_ref[...] = _rms(x_ref[...] + d_ref[...], g_ref[...])


def _final_norm(x, delta, g, row0, rows):
    d = x.shape[1]
    tr = _tile(_gcd(row0, rows) if row0 else rows, 256, 8)
    off = row0 // tr
    src = pl.BlockSpec((tr, d), lambda i: (off + i, 0))
    return pl.pallas_call(
        _final_kernel,
        grid=(rows // tr,),
        in_specs=[src, src, pl.BlockSpec((1, d), lambda i: (0, 0))],
        out_specs=pl.BlockSpec((tr, d), lambda i: (i, 0)),
        out_shape=jax.ShapeDtypeStruct((rows, d), F32),
        compiler_params=_cp(("arbitrary",)),
        name="final_norm",
    )(x, delta, g.reshape(1, d))


def _ffn_kernel(h_ref, wg_ref, wu_ref, wo_ref, o_ref):
    @pl.when(pl.program_id(1) == 0)
    def _():
        o_ref[...] = jnp.zeros_like(o_ref)

    h = h_ref[...]
    gate = jnp.dot(h, wg_ref[...].astype(BF16), preferred_element_type=F32)
    up = jnp.dot(h, wu_ref[...].astype(BF16), preferred_element_type=F32)
    act = (gate * jax.nn.sigmoid(gate)) * up * 0.5
    o_ref[...] += jnp.dot(act.astype(BF16), wo_ref[...].astype(BF16), preferred_element_type=F32)


def _ffn(h, w_in, w_out):
    n, d = h.shape
    f = w_out.shape[0]
    tm = _tile(n, 1100)
    tf = _tile(f, 128, V7X_LANES)
    nf = f // tf
    resident = pl.Buffered(1)
    return pl.pallas_call(
        _ffn_kernel,
        grid=(n // tm, nf),
        in_specs=[
            pl.BlockSpec((tm, d), lambda i, j: (i, 0), pipeline_mode=resident),
            pl.BlockSpec((d, tf), lambda i, j: (0, j)),
            pl.BlockSpec((d, tf), lambda i, j: (0, nf + j)),
            pl.BlockSpec((tf, d), lambda i, j: (j, 0)),
        ],
        out_specs=pl.BlockSpec((tm, d), lambda i, j: (i, 0), pipeline_mode=resident),
        out_shape=jax.ShapeDtypeStruct((n, d), F32),
        compiler_params=_cp(("arbitrary", "arbitrary")),
        name="ffn",
    )(h, w_in, w_in, w_out)


def _mm_kernel(x_ref, w_ref, o_ref, *, act):
    acc = jnp.dot(x_ref[...], w_ref[...].astype(BF16), preferred_element_type=F32)
    if act == "sigmoid":
        acc = jax.nn.sigmoid(acc)
    o_ref[...] = acc.astype(o_ref.dtype)


def _mm_res_kernel(x_ref, w_ref, r_ref, o_ref):
    o_ref[...] = r_ref[...] + jnp.dot(x_ref[...], w_ref[...].astype(BF16), preferred_element_type=F32)


def _matmul(x, w, *, col0=0, cols=None, out_dtype=F32, act=None, residual=None, tn=512):
    m, k = x.shape
    cols = w.shape[1] - col0 if cols is None else cols
    tn = min(tn, _round_up(cols, V7X_LANES))
    assert col0 % tn == 0
    nb, jb = -(-cols // tn), col0 // tn
    tm = _tile(m, 1100)
    in_specs = [pl.BlockSpec((tm, k), lambda i, j: (i, 0)),
                pl.BlockSpec((k, tn), lambda i, j: (0, jb + j))]
    args = [x, w]
    if residual is None:
        kern = functools.partial(_mm_kernel, act=act)
    else:
        kern = _mm_res_kernel
        in_specs.append(pl.BlockSpec((tm, tn), lambda i, j: (i, j)))
        args.append(residual)
    return pl.pallas_call(
        kern,
        grid=(m // tm, nb),
        in_specs=in_specs,
        out_specs=pl.BlockSpec((tm, tn), lambda i, j: (i, j)),
        out_shape=jax.ShapeDtypeStruct((m, nb * tn), out_dtype),
        compiler_params=_cp(("arbitrary", "arbitrary")),
        name="matmul",
    )(*args)


def _shift_rows(x, k, tile_row0, seq_len, halo):
    rows = lax.broadcasted_iota(jnp.int32, x.shape, 0)
    out = pltpu.roll(x, k, axis=0)
    at_start = (tile_row0 % seq_len) == 0
    for q in range(k):
        h = halo[V7X_SUBLANES - k + q:V7X_SUBLANES - k + q + 1, :]
        h = jnp.where(at_start, jnp.zeros_like(h), h)
        out = jnp.where(rows == q, h, out)
    return out


def _rwkv_pre_kernel(*refs, d_a, low_valid, tr, seq_len, mode):
    x_ref, aux_ref, mu_ref, w0_ref, a0_ref, lw_ref, la_ref, lg_ref = refs[-14:-6]
    r_ref, w_ref, k_ref, v_ref, a_ref, g_ref = refs[-6:]
    x = x_ref[...]
    if mode == "halo":
        prev = _shift_rows(x, 1, pl.program_id(0) * tr, seq_len, aux_ref[...])
    else:
        t = lax.broadcasted_iota(jnp.int32, x.shape, 0) % seq_len
        prev = jnp.where(t == 0, aux_ref[...], pltpu.roll(x, 1, axis=0))
    ps = x + (prev - x) * mu_ref[...]
    r_ref[...] = ps[:, :d_a]
    k_ref[...] = ps[:, d_a:2 * d_a]
    v_ref[...] = ps[:, 2 * d_a:3 * d_a]
    low = ps[:, 3 * d_a:]
    low = jnp.where(lax.broadcasted_iota(jnp.int32, low.shape, 1) < low_valid, low, 0.0)
    z = w0_ref[...] + jnp.dot(jnp.tanh(low).astype(BF16), lw_ref[...], preferred_element_type=F32)
    w_ref[...] = jnp.exp(-jnp.exp(-W_LOG_SHIFT) * jax.nn.sigmoid(z))
    a_ref[...] = jax.nn.sigmoid(
        a0_ref[...] + jnp.dot(low.astype(BF16), la_ref[...], preferred_element_type=F32))
    g_ref[...] = jnp.dot(jax.nn.sigmoid(low).astype(BF16), lg_ref[...], preferred_element_type=F32)


def _rwkv_pre(p_a, row0, rows, seq_len, first, params, low_valid, prev_out):
    n, pw = p_a.shape
    mu, w0, a0, lw, la, lg = params
    d_a = w0.shape[1]
    mode = "halo" if first is None else "first"
    if mode == "halo":
        tr = _tile(_gcd(seq_len, rows), 128, V7X_SUBLANES)
    else:
        tr = _tile(rows, 128, _lcm(seq_len, V7X_SUBLANES))
    assert row0 % tr == 0 and rows % tr == 0
    off = row0 // tr
    x_spec = pl.BlockSpec((tr, pw), lambda i: (off + i, 0))
    if mode == "halo":
        hb = tr // V7X_SUBLANES
        aux, aux_spec = p_a, pl.BlockSpec(
            (V7X_SUBLANES, pw), lambda i: (jnp.maximum((off + i) * hb - 1, 0), 0))
    else:
        aux, aux_spec = first, pl.BlockSpec((tr, pw), lambda i: (i, 0))
    full = lambda a: pl.BlockSpec(a.shape, lambda i: (0, 0))
    in_specs = [x_spec, aux_spec, full(mu), full(w0), full(a0), full(lw), full(la), full(lg)]
    args = [p_a, aux, mu, w0, a0, lw, la, lg]
    aliases = {}
    if prev_out is not None:
        in_specs = [pl.BlockSpec(memory_space=pl.ANY)] * 6 + in_specs
        args = list(prev_out) + args
        aliases = {q: q for q in range(6)}
    out_spec = pl.BlockSpec((tr, d_a), lambda i: (off + i, 0))
    return pl.pallas_call(
        functools.partial(_rwkv_pre_kernel, d_a=d_a, low_valid=low_valid, tr=tr, seq_len=seq_len,
                          mode=mode),
        grid=(rows // tr,),
        in_specs=in_specs,
        out_specs=[out_spec] * 6,
        out_shape=[jax.ShapeDtypeStruct((n, d_a), F32)] * 6,
        input_output_aliases=aliases,
        compiler_params=_cp(("arbitrary",)),
        name="rwkv_pre_" + mode,
    )(*args)


def _scan_kernel(r_ref, w_ref, k_ref, v_ref, a_ref, kk_ref, ka_ref, rk_ref, lnw_ref, lnb_ref,
                 s0_ref, y_ref, s_ref, av_ref, bv_ref, km_ref, *, tc):
    hd = s_ref.shape[1]

    @pl.when(pl.program_id(1) == 0)
    def _():
        s_ref[...] = s0_ref[...]

    def step(t, carry):
        k = k_ref[0, t]
        a = a_ref[0, t]
        v = v_ref[0, t]
        kk = k * kk_ref[...]
        norm = jnp.sqrt(jnp.sum(kk * kk, axis=0, keepdims=True))
        kk = kk * (1.0 / jnp.maximum(norm, 1e-12))
        kmod = k * (1.0 + (a - 1.0) * ka_ref[...])
        av_ref[...] = -kk
        bv_ref[...] = kk * a
        km_ref[...] = kmod
        sa = jnp.zeros_like(v)
        for j in range(hd):
            sa = sa + s_ref[0, j] * av_ref[pl.ds(j, 1), :]
        y = jnp.zeros_like(v)
        for j in range(hd):
            row = pl.ds(j, 1)
            sj = (s_ref[0, j] * w_ref[0, t, row, :] + sa * bv_ref[row, :]) + v * km_ref[row, :]
            s_ref[0, j] = sj
            y = y + sj * r_ref[0, t, row, :]
        mu = jnp.mean(y, axis=0, keepdims=True)
        yc = y - mu
        var = jnp.mean(yc * yc, axis=0, keepdims=True)
        yn = (yc * lax.rsqrt(var + GN_EPS)) * lnw_ref[...] + lnb_ref[...]
        bonus = jnp.sum(r_ref[0, t] * kmod * rk_ref[...], axis=0, keepdims=True)
        y_ref[0, t] = yn + bonus * v
        return carry

    lax.fori_loop(0, tc, step, 0)


def _scan(r, w, k, v, a, params, s0):
    g, t, hd, ln = r.shape
    tc = _tile(t, 64, 1)
    seq = pl.BlockSpec((1, tc, hd, ln), lambda gi, c: (gi, c, 0, 0))
    par = pl.BlockSpec((hd, ln), lambda gi, c: (0, 0))
    st = pl.BlockSpec((1, hd, hd, ln), lambda gi, c: (gi, 0, 0, 0))
    return pl.pallas_call(
        functools.partial(_scan_kernel, tc=tc),
        grid=(g, t // tc),
        in_specs=[seq] * 5 + [par] * 5 + [st],
        out_specs=[seq, st],
        out_shape=[jax.ShapeDtypeStruct(r.shape, F32), jax.ShapeDtypeStruct(s0.shape, F32)],
        scratch_shapes=[pltpu.VMEM((hd, ln), F32)] * 3,
        compiler_params=_cp(("arbitrary", "arbitrary")),
        name="rwkv_scan",
    )(r, w, k, v, a, *params, s0)


def _to_heads_on_lanes(x2d, b, t, bg):
    h = x2d.shape[1] // HEAD_A
    x = x2d.reshape(b // bg, bg, t, h, HEAD_A)
    return x.transpose(0, 2, 4, 1, 3).reshape(b // bg, t, HEAD_A, bg * h)


def _from_heads_on_lanes(x, b, t, bg):
    h = x.shape[-1] // bg
    x = x.reshape(b // bg, t, HEAD_A, bg, h)
    return x.transpose(0, 3, 1, 4, 2).reshape(b * t, h * HEAD_A)


def _param_on_lanes(p, bg):
    h = p.size // HEAD_A
    return jnp.tile(p.reshape(h, HEAD_A).T, (1, bg))


def _rwkv_group(rwka, row0, b, t, s_prev, scan_params):
    d_a = rwka[0].shape[1]
    h = d_a // HEAD_A
    bg = max(1, min(b, V7X_LANES // h))
    seqs = [_to_heads_on_lanes(x[row0:row0 + b * t], b, t, bg) for x in rwka]
    params = [_param_on_lanes(p, bg) for p in scan_params]
    if s_prev is None:
        s0 = jnp.zeros((b // bg, HEAD_A, HEAD_A, bg * h), F32)
    else:
        s0 = s_prev.astype(F32).reshape(b // bg, bg, h, HEAD_A, HEAD_A)
        s0 = s0.transpose(0, 4, 3, 1, 2).reshape(b // bg, HEAD_A, HEAD_A, bg * h)
    y, s_new = _scan(*seqs, params, s0)
    s_new = s_new.reshape(b // bg, HEAD_A, HEAD_A, bg, h).transpose(0, 3, 4, 2, 1)
    return _from_heads_on_lanes(y, b, t, bg), s_new.reshape(b, h, HEAD_A, HEAD_A)


def _gate_kernel(*refs):
    y_ref, g_ref, o_ref = refs[-3:]
    o_ref[...] = (y_ref[...] * g_ref[...]).astype(o_ref.dtype)


def _gate_rows(y, g, row0, prev_out):
    rows, d_a = y.shape
    n = g.shape[0]
    tr = _tile(_gcd(row0, rows) if row0 else rows, 512, 16)
    off = row0 // tr
    in_specs = [pl.BlockSpec((tr, d_a), lambda i: (i, 0)),
                pl.BlockSpec((tr, d_a), lambda i: (off + i, 0))]
    args, aliases = [y, g], {}
    if prev_out is not None:
        in_specs = [pl.BlockSpec(memory_space=pl.ANY)] + in_specs
        args, aliases = [prev_out] + args, {0: 0}
    return pl.pallas_call(
        _gate_kernel,
        grid=(rows // tr,),
        in_specs=in_specs,
        out_specs=pl.BlockSpec((tr, d_a), lambda i: (off + i, 0)),
        out_shape=jax.ShapeDtypeStruct((n, d_a), BF16),
        input_output_aliases=aliases,
        compiler_params=_cp(("arbitrary",)),
        name="rwkv_gate",
    )(*args)


def _conv_kernel(*refs, tr, seq_len, mode):
    gb_ref, gc_ref, u_ref, aux1_ref, aux2_ref, cw_ref = refs[-8:-2]
    y_ref, u2_ref = refs[-2:]
    f1_ref, f2_ref = aux1_ref, aux2_ref
    u2 = gc_ref[...] * u_ref[...]
    if mode == "halo":
        halo = aux1_ref[...] * aux2_ref[...]
        row0 = pl.program_id(0) * tr
        p1 = _shift_rows(u2, 1, row0, seq_len, halo)
        p2 = _shift_rows(u2, 2, row0, seq_len, halo)
    else:
        t = lax.broadcasted_iota(jnp.int32, u2.shape, 0) % seq_len
        p1 = jnp.where(t < 1, f1_ref[...], pltpu.roll(u2, 1, axis=0))
        p2 = jnp.where(t < 2, f2_ref[...], pltpu.roll(u2, 2, axis=0))
    z = p2 * cw_ref[0:1, :] + p1 * cw_ref[1:2, :]
    z = z + u2 * cw_ref[2:3, :]
    y_ref[...] = (gb_ref[...] * z).astype(y_ref.dtype)
    u2_ref[...] = u2


def _conv(p_bc, row0, rows, seq_len, firsts, conv_w, prev_out):
    n = p_bc.shape[0]
    d_b = conv_w.shape[1]
    mode = "halo" if firsts is None else "first"
    if mode == "halo":
        tr = _tile(_gcd(seq_len, rows), 256, 16)
    else:
        tr = _tile(rows, 256, _lcm(seq_len, 16))
    assert row0 % tr == 0 and rows % tr == 0 and seq_len >= CONV_W - 1
    off = row0 // tr
    col = lambda c: pl.BlockSpec((tr, d_b), lambda i: (off + i, c))
    in_specs = [col(0), col(1), col(2)]
    args = [p_bc, p_bc, p_bc]
    if mode == "halo":
        hb = tr // V7X_SUBLANES
        hcol = lambda c: pl.BlockSpec(
            (V7X_SUBLANES, d_b), lambda i: (jnp.maximum((off + i) * hb - 1, 0), c))
        in_specs += [hcol(1), hcol(2)]
        args += [p_bc, p_bc]
    else:
        in_specs += [pl.BlockSpec((tr, d_b), lambda i: (i, 0))] * 2
        args += list(firsts)
    in_specs.append(pl.BlockSpec(conv_w.shape, lambda i: (0, 0)))
    args.append(conv_w)
    aliases = {}
    if prev_out is not None:
        in_specs = [pl.BlockSpec(memory_space=pl.ANY)] * 2 + in_specs
        args = list(prev_out) + args
        aliases = {0: 0, 1: 1}
    out_spec = pl.BlockSpec((tr, d_b), lambda i: (off + i, 0))
    return pl.pallas_call(
        functools.partial(_conv_kernel, tr=tr, seq_len=seq_len, mode=mode),
        grid=(rows // tr,),
        in_specs=in_specs,
        out_specs=[out_spec, out_spec],
        out_shape=[jax.ShapeDtypeStruct((n, d_b), BF16), jax.ShapeDtypeStruct((n, d_b), F32)],
        input_output_aliases=aliases,
        compiler_params=_cp(("arbitrary",)),
        name="conv_" + mode,
    )(*args)


def _attn_kernel(*refs, bb, tq, heads):
    q_ref, k_ref, v_ref, o_ref = refs[-4:]
    dh = k_ref.shape[2] // heads
    scale = dh ** -0.5
    row_batch = lax.broadcasted_iota(jnp.int32, (bb * tq, dh), 0) // tq
    for h in range(heads):
        cols = slice(h * dh, (h + 1) * dh)
        q = q_ref[:, cols].astype(BF16)
        out = None
        for b in range(bb):
            k = k_ref[b, :, cols].astype(BF16)
            v = v_ref[b, :, cols].astype(BF16)
            s = lax.dot_general(q, k, (((1,), (1,)), ((), ())), preferred_element_type=F32) * scale
            p = jnp.exp(s - jnp.max(s, axis=-1, keepdims=True))
            p = p / jnp.sum(p, axis=-1, keepdims=True)
            o = jnp.dot(p.astype(BF16), v, preferred_element_type=F32)
            out = o if out is None else jnp.where(row_batch == b, o, out)
        o_ref[:, cols] = out.astype(o_ref.dtype)


def _attention(q_src, q_col, row0, b, t, mem_k, mem_v, heads, out_rows, prev_out):
    n_mem, d_c = mem_k.shape[1:]
    if t >= 128:
        bb, tq = 1, _tile(t, 512, 16)
    else:
        kv_block_cap = max(1, ATTN_KV_BLOCK_BYTES // (n_mem * d_c * 4))
        bb, tq = _tile(b, min(kv_block_cap, max(1, 64 // t)), 1), t
    assert (bb * tq) % 16 == 0 and row0 % (bb * tq) == 0
    tpb = t // tq
    off = row0 // (bb * tq)
    qmap = lambda i, j: (off + i * tpb + j, q_col)
    omap = lambda i, j: (off + i * tpb + j, 0)
    kv = pl.BlockSpec((bb, n_mem, d_c), lambda i, j: (i, 0, 0))
    in_specs = [pl.BlockSpec((bb * tq, d_c), qmap), kv, kv]
    args, aliases = [q_src, mem_k, mem_v], {}
    if prev_out is not None:
        in_specs = [pl.BlockSpec(memory_space=pl.ANY)] + in_specs
        args, aliases = [prev_out] + args, {0: 0}
    return pl.pallas_call(
        functools.partial(_attn_kernel, bb=bb, tq=tq, heads=heads),
        grid=(b // bb, tpb),
        in_specs=in_specs,
        out_specs=pl.BlockSpec((bb * tq, d_c), omap),
        out_shape=jax.ShapeDtypeStruct((out_rows, d_c), BF16),
        input_output_aliases=aliases,
        compiler_params=_cp(("arbitrary", "arbitrary")),
        name="mem_attention",
    )(*args)


def _merge_kernel(ya_ref, yb_ref, yc_ref, ga_ref, gb_ref, gc_ref, wa_ref, wb_ref, wc_ref, o_ref):
    def branch(y_ref, g_ref, w_ref):
        return g_ref[...].astype(F32) * jnp.dot(
            y_ref[...], w_ref[...].astype(BF16), preferred_element_type=F32)

    acc = branch(ya_ref, ga_ref, wa_ref) + branch(yb_ref, gb_ref, wb_ref)
    o_ref[...] = (acc + branch(yc_ref, gc_ref, wc_ref)).astype(o_ref.dtype)


def _merge(ya, yb, yc, gates, wa, wb, wc):
    n, d = ya.shape[0], wa.shape[1]
    tm = _tile(n, 1100)
    tn = _tile(d, 512, V7X_LANES)
    nj = d // tn
    act = lambda y: pl.BlockSpec((tm, y.shape[1]), lambda i, j: (i, 0))
    gate = lambda c: pl.BlockSpec((tm, tn), lambda i, j: (i, c * nj + j))
    wsp = lambda w: pl.BlockSpec((w.shape[0], tn), lambda i, j: (0, j))
    return pl.pallas_call(
        _merge_kernel,
        grid=(n // tm, nj),
        in_specs=[act(ya), act(yb), act(yc), gate(0), gate(1), gate(2), wsp(wa), wsp(wb), wsp(wc)],
        out_specs=pl.BlockSpec((tm, tn), lambda i, j: (i, j)),
        out_shape=jax.ShapeDtypeStruct((n, d), BF16),
        compiler_params=_cp(("arbitrary", "arbitrary")),
        name="branch_merge",
    )(ya, yb, yc, gates, gates, gates, wa, wb, wc)


def _pad_rows(w, row0, total):
    return jnp.zeros((total, w.shape[1]), BF16).at[row0:row0 + w.shape[0]].set(w.astype(BF16))


def _first_rows(state, seq_len, width):
    b, k, c = state.shape
    out = jnp.zeros((b, seq_len, width), F32).at[:, :k, :c].set(state.astype(F32))
    return out.reshape(b * seq_len, width)


def _layer(x, h1, groups, lp, mem_kv):
    n, d = x.shape
    d_a = lp["w0"].shape[0]
    d_b = lp["conv_w"].shape[1]
    p_a_w = lp["mu_shift"].shape[0]
    lora_w, lora_a, lora_g = (lp[q].shape[0] for q in ("w_lora_up", "a_lora_up", "g_lora_up"))
    heads_c = mem_kv[0][0].shape[2]
    d_c = heads_c * mem_kv[0][0].shape[3]
    assert d_c == d_b and p_a_w == 3 * d_a + lora_w + lora_a + lora_g

    delta = _ffn(h1, lp["w_ffn1_in"], lp["w_ffn1_out"])
    x, h = _add_norm(x, delta, lp["g_mix"])

    w_in = lp["w_in"]
    p_a = _matmul(h, w_in, col0=0, cols=p_a_w)
    w_rest = w_in[:, p_a_w:]
    p_bc = _matmul(h, w_rest, col0=0, cols=3 * d_b + d_c)
    gates = _matmul(h, w_rest, col0=3 * d_b + d_c, cols=3 * d, out_dtype=BF16, act="sigmoid")

    pw = p_a.shape[1]
    low_w = pw - 3 * d_a
    pre_params = (
        jnp.pad(lp["mu_shift"], (0, pw - p_a_w)).reshape(1, pw),
        lp["w0"].reshape(1, d_a), lp["a0"].reshape(1, d_a),
        _pad_rows(lp["w_lora_up"], 0, low_w),
        _pad_rows(lp["a_lora_up"], lora_w, low_w),
        _pad_rows(lp["g_lora_up"], lora_w + lora_a, low_w),
    )
    scan_params = (lp["k_k"], lp["k_a"], lp["r_k"].reshape(-1), lp["lnx_w"], lp["lnx_b"])
    pre = None
    for row0, b, t, st in groups:
        first = None if st is None else _first_rows(st["shift"][:, None, :], t, pw)
        pre = _rwkv_pre(p_a, row0, b * t, t, first, pre_params, p_a_w - 3 * d_a, pre)
    r, w, k, v, a, g = pre
    ya, new_states = None, []
    for row0, b, t, st in groups:
        y, s_new = _rwkv_group((r, w, k, v, a), row0, b, t, None if st is None else st["rwkv"], scan_params)
        ya = _gate_rows(y, g, row0, ya)
        shift_new = p_a[row0:row0 + b * t].reshape(b, t, pw)[:, -1, :p_a_w]
        new_states.append({"rwkv": s_new, "shift": shift_new})

    conv = None
    for row0, b, t, st in groups:
        firsts = None
        if st is not None:
            buf = st["conv"]
            firsts = (_first_rows(buf[:, 1:, :], t, d_b), _first_rows(buf, t, d_b))
        conv = _conv(p_bc, row0, b * t, t, firsts, lp["conv_w"], conv)
    yb, u2 = conv
    for (row0, b, t, st), ns in zip(groups, new_states):
        ns["conv"] = u2[row0:row0 + b * t].reshape(b, t, d_b)[:, -(CONV_W - 1):, :]

    yc = None
    for (row0, b, t, st), (mk, mv) in zip(groups, mem_kv):
        n_mem = mk.shape[1]
        yc = _attention(p_bc, (3 * d_b) // d_c, row0, b, t,
                        mk.reshape(b, n_mem, d_c), mv.reshape(b, n_mem, d_c), heads_c, n, yc)

    merged = _merge(ya, yb, yc, gates, lp["w_br_a"], lp["w_br_b"], lp["w_br_c"])
    x = _matmul(merged, lp["w_out"], residual=x)

    h = _norm(x, lp["g_ffn2"])
    delta = _ffn(h, lp["w_ffn2_in"], lp["w_ffn2_out"])
    return x, delta, new_states


def kernel(x_prompt, x_sample, mem_prompt, cache_mem_k, cache_mem_v, state_rwkv, state_shift, state_conv, g_ffn1, w_ffn1_in, w_ffn1_out, g_mix, w_in, mu_shift, w0, w_lora_up, a0, a_lora_up, g_lora_up, k_k, k_a, r_k, lnx_w, lnx_b, conv_w, g_mem, w_mem_kv, w_br_a, w_br_b, w_br_c, w_out, g_ffn2, w_ffn2_in, w_ffn2_out, g_final):
    bp, tp, d = x_prompt.shape
    bs, ts, _ = x_sample.shape
    n_p, n_s = bp * tp, bs * ts
    depth = w_in.shape[0]
    n_mem, heads_c, dh_c = cache_mem_k.shape[2:]
    d_c = heads_c * dh_c

    per_layer = dict(
        g_ffn1=g_ffn1, w_ffn1_in=w_ffn1_in, w_ffn1_out=w_ffn1_out, g_mix=g_mix, w_in=w_in,
        mu_shift=mu_shift, w0=w0, w_lora_up=w_lora_up, a0=a0, a_lora_up=a_lora_up,
        g_lora_up=g_lora_up, k_k=k_k, k_a=k_a, r_k=r_k, lnx_w=lnx_w, lnx_b=lnx_b, conv_w=conv_w,
        w_br_a=w_br_a, w_br_b=w_br_b, w_br_c=w_br_c, w_out=w_out, g_ffn2=g_ffn2,
        w_ffn2_in=w_ffn2_in, w_ffn2_out=w_ffn2_out)

    mem_rows = mem_prompt.reshape(bp * n_mem, d)
    outs = {q: [] for q in ("mk", "mv", "rp", "shp", "cp", "rs", "shs", "cs")}
    x = delta = None
    for l in range(depth):
        lp = {q: p[l] for q, p in per_layer.items()}
        if l == 0:
            x, h1 = _norm_concat(x_prompt.reshape(n_p, d), x_sample.reshape(n_s, d), lp["g_ffn1"])
        else:
            x, h1 = _add_norm(x, delta, lp["g_ffn1"])
        kv = _matmul(_norm(mem_rows, g_mem[l]), w_mem_kv[l])
        mk_p = kv[:, :d_c].reshape(bp, n_mem, heads_c, dh_c)
        mv_p = kv[:, d_c:2 * d_c].reshape(bp, n_mem, heads_c, dh_c)
        groups = [
            (0, bp, tp, None),
            (n_p, bs, ts, {"rwkv": state_rwkv[l], "shift": state_shift[l], "conv": state_conv[l]}),
        ]
        x, delta, (st_p, st_s) = _layer(
            x, h1, groups, lp, [(mk_p, mv_p), (cache_mem_k[l], cache_mem_v[l])])
        outs["mk"].append(mk_p)
        outs["mv"].append(mv_p)
        for tag, st in (("p", st_p), ("s", st_s)):
            outs["r" + tag].append(st["rwkv"])
            outs["sh" + tag].append(st["shift"])
            outs["c" + tag].append(st["conv"])
    y_prompt = _final_norm(x, delta, g_final, 0, n_p).reshape(bp, tp, d)
    y_sample = _final_norm(x, delta, g_final, n_p, n_s).reshape(bs, ts, d)
    stack = lambda q: jnp.stack(outs[q], 0)
    return (y_prompt, y_sample, stack("mk"), stack("mv"), stack("rp"), stack("shp"), stack("cp"),
            stack("rs"), stack("shs"), stack("cs"))
```

```python
import functools

import jax
import jax.numpy as jnp
from jax import lax
from jax.experimental import pallas as pl
from jax.experimental.pallas import tpu as pltpu

F32 = jnp.float32
BF16 = jnp.bfloat16

EPS = 1e-6
GN_EPS = 64e-5
HEAD_A = 64
CONV_W = 3
V7X_LANES = 128
V7X_SUBLANES = 8
V7X_VMEM_LIMIT = 56 * 1024 * 1024
W_LOG_SHIFT = 0.5
ATTN_KV_BLOCK_BYTES = 4 * 1024 * 1024


def _cp(sem, vmem=V7X_VMEM_LIMIT):
    return pltpu.CompilerParams(dimension_semantics=sem, vmem_limit_bytes=vmem)


def _tile(n, cap, mult=16):
    best = None
    for t in range(mult, min(n, cap) + 1, mult):
        if n % t == 0:
            best = t
    return best if best is not None else n


def _round_up(n, m):
    return (n + m - 1) // m * m


def _rms(x, g):
    ms = jnp.mean(x * x, axis=-1, keepdims=True)
    return (x * lax.rsqrt(ms + EPS)) * g


def _norm2_kernel(xp_ref, xs_ref, g_ref, x_ref, h_ref, *, n_prompt_tiles):
    i = pl.program_id(0)

    def emit(x):
        x_ref[...] = x
        h_ref[...] = _rms(x, g_ref[...]).astype(BF16)

    @pl.when(i < n_prompt_tiles)
    def _():
        emit(xp_ref[...])

    @pl.when(i >= n_prompt_tiles)
    def _():
        emit(xs_ref[...])


def _norm_concat(xp, xs, g):
    (n_p, d), n_s = xp.shape, xs.shape[0]
    tr = _tile(_gcd(n_p, n_s), 256, 8)
    npt, nst = n_p // tr, n_s // tr
    n = n_p + n_s
    return pl.pallas_call(
        functools.partial(_norm2_kernel, n_prompt_tiles=npt),
        grid=(npt + nst,),
        in_specs=[
            pl.BlockSpec((tr, d), lambda i: (jnp.minimum(i, npt - 1), 0)),
            pl.BlockSpec((tr, d), lambda i: (jnp.maximum(i - npt, 0), 0)),
            pl.BlockSpec((1, d), lambda i: (0, 0)),
        ],
        out_specs=[pl.BlockSpec((tr, d), lambda i: (i, 0)),
                   pl.BlockSpec((tr, d), lambda i: (i, 0))],
        out_shape=[jax.ShapeDtypeStruct((n, d), F32), jax.ShapeDtypeStruct((n, d), BF16)],
        compiler_params=_cp(("arbitrary",)),
        name="norm_concat",
    )(xp, xs, g.reshape(1, d))


def _gcd(a, b):
    while b:
        a, b = b, a % b
    return a


def _lcm(a, b):
    return a * b // _gcd(a, b)


def _norm_kernel(x_ref, g_ref, h_ref):
    h_ref[...] = _rms(x_ref[...], g_ref[...]).astype(h_ref.dtype)


def _norm(x, g, out_dtype=BF16, row0=0, rows=None):
    n, d = x.shape
    rows = n if rows is None else rows
    tr = _tile(_gcd(row0, rows) if row0 else rows, 256, 16)
    off = row0 // tr
    return pl.pallas_call(
        _norm_kernel,
        grid=(rows // tr,),
        in_specs=[pl.BlockSpec((tr, d), lambda i: (off + i, 0)),
                  pl.BlockSpec((1, d), lambda i: (0, 0))],
        out_specs=pl.BlockSpec((tr, d), lambda i: (i, 0)),
        out_shape=jax.ShapeDtypeStruct((rows, d), out_dtype),
        compiler_params=_cp(("arbitrary",)),
        name="norm",
    )(x, g.reshape(1, d))


def _gateup_kernel(h_ref, wg_ref, wu_ref, o_ref):
    h = h_ref[...]
    gate = jnp.dot(h, wg_ref[...].astype(BF16), preferred_element_type=F32)
    up = jnp.dot(h, wu_ref[...].astype(BF16), preferred_element_type=F32)
    o_ref[...] = ((gate * jax.nn.sigmoid(gate)) * up * 0.5).astype(o_ref.dtype)


def _gateup(h, w_in):
    n, d = h.shape
    f = w_in.shape[1] // 2
    tm = _tile(n, 1100)
    tn = _tile(f, 256, V7X_LANES)
    nf = f // tn
    return pl.pallas_call(
        _gateup_kernel,
        grid=(n // tm, nf),
        in_specs=[
            pl.BlockSpec((tm, d), lambda i, j: (i, 0)),
            pl.BlockSpec((d, tn), lambda i, j: (0, j)),
            pl.BlockSpec((d, tn), lambda i, j: (0, nf + j)),
        ],
        out_specs=pl.BlockSpec((tm, tn), lambda i, j: (i, j)),
        out_shape=jax.ShapeDtypeStruct((n, f), BF16),
        compiler_params=_cp(("arbitrary", "arbitrary")),
        name="ffn_gateup",
    )(h, w_in, w_in)


def _mm_kernel(x_ref, w_ref, o_ref, *, act):
    acc = jnp.dot(x_ref[...], w_ref[...].astype(BF16), preferred_element_type=F32)
    if act == "sigmoid":
        acc = jax.nn.sigmoid(acc)
    o_ref[...] = acc.astype(o_ref.dtype)


def _mm_res_kernel(x_ref, w_ref, r_ref, o_ref):
    o_ref[...] = r_ref[...] + jnp.dot(x_ref[...], w_ref[...].astype(BF16), preferred_element_type=F32)


def _mm_shift_kernel(x_ref, wa_ref, wb_ref, o_ref, w_scr, *, shift, act):
    @pl.when(pl.program_id(1) == 0)
    def _():
        w = jnp.concatenate([wa_ref[:, shift:], wb_ref[:, :w_scr.shape[1] - (V7X_LANES - shift)]], axis=1)
        w_scr[...] = w.astype(BF16)

    acc = jnp.dot(x_ref[...], w_scr[...], preferred_element_type=F32)
    if act == "sigmoid":
        acc = jax.nn.sigmoid(acc)
    o_ref[...] = acc.astype(o_ref.dtype)


def _matmul_shifted(x, w, *, col0, cols, out_dtype=F32, act=None, tn=512):
    m, k = x.shape
    shift = col0 % V7X_LANES
    base = col0 - shift
    assert shift and cols % tn == 0 and (base + V7X_LANES) % tn == 0
    ja, jb = base // V7X_LANES, (base + V7X_LANES) // tn
    step = tn // V7X_LANES
    tm = _tile(m, 1100)
    return pl.pallas_call(
        functools.partial(_mm_shift_kernel, shift=shift, act=act),
        grid=(cols // tn, m // tm),
        in_specs=[pl.BlockSpec((tm, k), lambda j, i: (i, 0)),
                  pl.BlockSpec((k, V7X_LANES), lambda j, i: (0, ja + step * j)),
                  pl.BlockSpec((k, tn), lambda j, i: (0, jb + j))],
        out_specs=pl.BlockSpec((tm, tn), lambda j, i: (i, j)),
        out_shape=jax.ShapeDtypeStruct((m, cols), out_dtype),
        scratch_shapes=[pltpu.VMEM((k, tn), BF16)],
        compiler_params=_cp(("arbitrary", "arbitrary")),
        name="matmul_shifted",
    )(x, w, w)


def _matmul(x, w, *, col0=0, cols=None, out_dtype=F32, act=None, residual=None, tn=512,
            x_resident=False):
    m, k = x.shape
    cols = w.shape[1] - col0 if cols is None else cols
    tn = min(tn, _round_up(cols, V7X_LANES))
    assert col0 % tn == 0
    nb, jb = -(-cols // tn), col0 // tn
    tm = _tile(m, 1100)
    x_mode = pl.Buffered(1) if x_resident else None
    in_specs = [pl.BlockSpec((tm, k), lambda i, j: (i, 0), pipeline_mode=x_mode),
                pl.BlockSpec((k, tn), lambda i, j: (0, jb + j))]
    args = [x, w]
    if residual is None:
        kern = functools.partial(_mm_kernel, act=act)
    else:
        kern = _mm_res_kernel
        in_specs.append(pl.BlockSpec((tm, tn), lambda i, j: (i, j)))
        args.append(residual)
    return pl.pallas_call(
        kern,
        grid=(m // tm, nb),
        in_specs=in_specs,
        out_specs=pl.BlockSpec((tm, tn), lambda i, j: (i, j)),
        out_shape=jax.ShapeDtypeStruct((m, nb * tn), out_dtype),
        compiler_params=_cp(("arbitrary", "arbitrary")),
        name="matmul",
    )(*args)


def _shift_rows(x, k, tile_row0, seq_len, halo):
    rows = lax.broadcasted_iota(jnp.int32, x.shape, 0)
    out = pltpu.roll(x, k, axis=0)
    at_start = (tile_row0 % seq_len) == 0
    for q in range(k):
        h = halo[V7X_SUBLANES - k + q:V7X_SUBLANES - k + q + 1, :]
        h = jnp.where(at_start, jnp.zeros_like(h), h)
        out = jnp.where(rows == q, h, out)
    return out


def _rwkv_pre_kernel(*refs, d_a, low_valid, tr, seq_len, mode):
    x_ref, aux_ref, mu_ref, w0_ref, a0_ref, lw_ref, la_ref, lg_ref = refs[-14:-6]
    r_ref, w_ref, k_ref, v_ref, a_ref, g_ref = refs[-6:]
    x = x_ref[...]
    if mode == "halo":
        prev = _shift_rows(x, 1, pl.program_id(0) * tr, seq_len, aux_ref[...])
    else:
        t = lax.broadcasted_iota(jnp.int32, x.shape, 0) % seq_len
        prev = jnp.where(t == 0, aux_ref[...], pltpu.roll(x, 1, axis=0))
    ps = x + (prev - x) * mu_ref[...]
    r_ref[...] = ps[:, :d_a]
    k_ref[...] = ps[:, d_a:2 * d_a]
    v_ref[...] = ps[:, 2 * d_a:3 * d_a]
    low = ps[:, 3 * d_a:]
    low = jnp.where(lax.broadcasted_iota(jnp.int32, low.shape, 1) < low_valid, low, 0.0)
    z = w0_ref[...] + jnp.dot(jnp.tanh(low).astype(BF16), lw_ref[...], preferred_element_type=F32)
    w_ref[...] = jnp.exp(-jnp.exp(-W_LOG_SHIFT) * jax.nn.sigmoid(z))
    a_ref[...] = jax.nn.sigmoid(
        a0_ref[...] + jnp.dot(low.astype(BF16), la_ref[...], preferred_element_type=F32))
    g_ref[...] = jnp.dot(jax.nn.sigmoid(low).astype(BF16), lg_ref[...], preferred_element_type=F32)


def _rwkv_pre(p_a, row0, rows, seq_len, first, params, low_valid):
    n, pw = p_a.shape
    mu, w0, a0, lw, la, lg = params
    d_a = w0.shape[1]
    mode = "halo" if first is None else "first"
    if mode == "halo":
        tr = _tile(_gcd(seq_len, rows), 128, V7X_SUBLANES)
    else:
        tr = _tile(rows, 128, _lcm(seq_len, V7X_SUBLANES))
    assert row0 % tr == 0 and rows % tr == 0
    off = row0 // tr
    x_spec = pl.BlockSpec((tr, pw), lambda i: (off + i, 0))
    if mode == "halo":
        hb = tr // V7X_SUBLANES
        aux, aux_spec = p_a, pl.BlockSpec(
            (V7X_SUBLANES, pw), lambda i: (jnp.maximum((off + i) * hb - 1, 0), 0))
    else:
        aux, aux_spec = first, pl.BlockSpec((tr, pw), lambda i: (i, 0))
    full = lambda a: pl.BlockSpec(a.shape, lambda i: (0, 0))
    in_specs = [x_spec, aux_spec, full(mu), full(w0), full(a0), full(lw), full(la), full(lg)]
    args = [p_a, aux, mu, w0, a0, lw, la, lg]
    out_spec = pl.BlockSpec((tr, d_a), lambda i: (i, 0))
    return pl.pallas_call(
        functools.partial(_rwkv_pre_kernel, d_a=d_a, low_valid=low_valid, tr=tr, seq_len=seq_len,
                          mode=mode),
        grid=(rows // tr,),
        in_specs=in_specs,
        out_specs=[out_spec] * 6,
        out_shape=[jax.ShapeDtypeStruct((rows, d_a), F32)] * 6,
        compiler_params=_cp(("arbitrary",)),
        name="rwkv_pre_" + mode,
    )(*args)


def _scan_kernel(r_ref, w_ref, k_ref, v_ref, a_ref, kk_ref, ka_ref, rk_ref, lnw_ref, lnb_ref,
                 s0_ref, y_ref, s_ref, av_ref, bv_ref, km_ref, *, tc):
    hd = s_ref.shape[1]

    @pl.when(pl.program_id(1) == 0)
    def _():
        s_ref[...] = s0_ref[...]

    def step(t, carry):
        k = k_ref[0, t]
        a = a_ref[0, t]
        v = v_ref[0, t]
        kk = k * kk_ref[...]
        norm = jnp.sqrt(jnp.sum(kk * kk, axis=0, keepdims=True))
        kk = kk * (1.0 / jnp.maximum(norm, 1e-12))
        kmod = k * (1.0 + (a - 1.0) * ka_ref[...])
        av_ref[...] = -kk
        bv_ref[...] = kk * a
        km_ref[...] = kmod
        sa = jnp.zeros_like(v)
        for j in range(hd):
            sa = sa + s_ref[0, j] * av_ref[pl.ds(j, 1), :]
        y = jnp.zeros_like(v)
        for j in range(hd):
            row = pl.ds(j, 1)
            sj = (s_ref[0, j] * w_ref[0, t, row, :] + sa * bv_ref[row, :]) + v * km_ref[row, :]
            s_ref[0, j] = sj
            y = y + sj * r_ref[0, t, row, :]
        mu = jnp.mean(y, axis=0, keepdims=True)
        yc = y - mu
        var = jnp.mean(yc * yc, axis=0, keepdims=True)
        yn = (yc * lax.rsqrt(var + GN_EPS)) * lnw_ref[...] + lnb_ref[...]
        bonus = jnp.sum(r_ref[0, t] * kmod * rk_ref[...], axis=0, keepdims=True)
        y_ref[0, t] = yn + bonus * v
        return carry

    lax.fori_loop(0, tc, step, 0)


def _scan(r, w, k, v, a, params, s0):
    g, t, hd, ln = r.shape
    tc = _tile(t, 64, 1)
    seq = pl.BlockSpec((1, tc, hd, ln), lambda gi, c: (gi, c, 0, 0))
    par = pl.BlockSpec((hd, ln), lambda gi, c: (0, 0))
    st = pl.BlockSpec((1, hd, hd, ln), lambda gi, c: (gi, 0, 0, 0))
    return pl.pallas_call(
        functools.partial(_scan_kernel, tc=tc),
        grid=(g, t // tc),
        in_specs=[seq] * 5 + [par] * 5 + [st],
        out_specs=[seq, st],
        out_shape=[jax.ShapeDtypeStruct(r.shape, F32), jax.ShapeDtypeStruct(s0.shape, F32)],
        scratch_shapes=[pltpu.VMEM((hd, ln), F32)] * 3,
        compiler_params=_cp(("arbitrary", "arbitrary")),
        name="rwkv_scan",
    )(r, w, k, v, a, *params, s0)


def _to_heads_on_lanes(x2d, b, t, bg):
    h = x2d.shape[1] // HEAD_A
    x = x2d.reshape(b // bg, bg, t, h, HEAD_A)
    return x.transpose(0, 2, 4, 1, 3).reshape(b // bg, t, HEAD_A, bg * h)


def _from_heads_on_lanes(x, b, t, bg):
    h = x.shape[-1] // bg
    x = x.reshape(b // bg, t, HEAD_A, bg, h)
    return x.transpose(0, 3, 1, 4, 2).reshape(b * t, h * HEAD_A)


def _param_on_lanes(p, bg):
    h = p.size // HEAD_A
    return jnp.tile(p.reshape(h, HEAD_A).T, (1, bg))


def _rwkv_group(rwka, b, t, s_prev, scan_params):
    d_a = rwka[0].shape[1]
    h = d_a // HEAD_A
    bg = max(1, min(b, V7X_LANES // h))
    seqs = [_to_heads_on_lanes(x, b, t, bg) for x in rwka]
    params = [_param_on_lanes(p, bg) for p in scan_params]
    if s_prev is None:
        s0 = jnp.zeros((b // bg, HEAD_A, HEAD_A, bg * h), F32)
    else:
        s0 = s_prev.astype(F32).reshape(b // bg, bg, h, HEAD_A, HEAD_A)
        s0 = s0.transpose(0, 4, 3, 1, 2).reshape(b // bg, HEAD_A, HEAD_A, bg * h)
    y, s_new = _scan(*seqs, params, s0)
    s_new = s_new.reshape(b // bg, HEAD_A, HEAD_A, bg, h).transpose(0, 3, 4, 2, 1)
    return _from_heads_on_lanes(y, b, t, bg), s_new.reshape(b, h, HEAD_A, HEAD_A)


def _gate_kernel(*refs):
    y_ref, g_ref, o_ref = refs[-3:]
    o_ref[...] = (y_ref[...] * g_ref[...]).astype(o_ref.dtype)


def _gate_rows(y, g, row0, n, prev_out):
    rows, d_a = y.shape
    tr = _tile(_gcd(row0, rows) if row0 else rows, 512, 16)
    off = row0 // tr
    in_specs = [pl.BlockSpec((tr, d_a), lambda i: (i, 0)),
                pl.BlockSpec((tr, d_a), lambda i: (i, 0))]
    args, aliases = [y, g], {}
    if prev_out is not None:
        in_specs = [pl.BlockSpec(memory_space=pl.ANY)] + in_specs
        args, aliases = [prev_out] + args, {0: 0}
    return pl.pallas_call(
        _gate_kernel,
        grid=(rows // tr,),
        in_specs=in_specs,
        out_specs=pl.BlockSpec((tr, d_a), lambda i: (off + i, 0)),
        out_shape=jax.ShapeDtypeStruct((n, d_a), BF16),
        input_output_aliases=aliases,
        compiler_params=_cp(("arbitrary",)),
        name="rwkv_gate",
    )(*args)


def _conv_kernel(*refs, tr, seq_len, mode):
    gb_ref, gc_ref, u_ref, aux1_ref, aux2_ref, cw_ref = refs[-8:-2]
    y_ref, u2_ref = refs[-2:]
    f1_ref, f2_ref = aux1_ref, aux2_ref
    u2 = gc_ref[...] * u_ref[...]
    if mode == "halo":
        halo = aux1_ref[...] * aux2_ref[...]
        row0 = pl.program_id(0) * tr
        p1 = _shift_rows(u2, 1, row0, seq_len, halo)
        p2 = _shift_rows(u2, 2, row0, seq_len, halo)
    else:
        t = lax.broadcasted_iota(jnp.int32, u2.shape, 0) % seq_len
        p1 = jnp.where(t < 1, f1_ref[...], pltpu.roll(u2, 1, axis=0))
        p2 = jnp.where(t < 2, f2_ref[...], pltpu.roll(u2, 2, axis=0))
    z = p2 * cw_ref[0:1, :] + p1 * cw_ref[1:2, :]
    z = z + u2 * cw_ref[2:3, :]
    y_ref[...] = (gb_ref[...] * z).astype(y_ref.dtype)
    u2_ref[...] = u2


def _conv(p_bc, row0, rows, seq_len, firsts, conv_w, prev_out):
    n = p_bc.shape[0]
    d_b = conv_w.shape[1]
    mode = "halo" if firsts is None else "first"
    if mode == "halo":
        tr = _tile(_gcd(seq_len, rows), 256, 16)
    else:
        tr = _tile(rows, 256, _lcm(seq_len, 16))
    assert row0 % tr == 0 and rows % tr == 0 and seq_len >= CONV_W - 1
    off = row0 // tr
    col = lambda c: pl.BlockSpec((tr, d_b), lambda i: (off + i, c))
    in_specs = [col(0), col(1), col(2)]
    args = [p_bc, p_bc, p_bc]
    if mode == "halo":
        hb = tr // V7X_SUBLANES
        hcol = lambda c: pl.BlockSpec(
            (V7X_SUBLANES, d_b), lambda i: (jnp.maximum((off + i) * hb - 1, 0), c))
        in_specs += [hcol(1), hcol(2)]
        args += [p_bc, p_bc]
    else:
        in_specs += [pl.BlockSpec((tr, d_b), lambda i: (i, 0))] * 2
        args += list(firsts)
    in_specs.append(pl.BlockSpec(conv_w.shape, lambda i: (0, 0)))
    args.append(conv_w)
    aliases = {}
    if prev_out is not None:
        in_specs = [pl.BlockSpec(memory_space=pl.ANY)] * 2 + in_specs
        args = list(prev_out) + args
        aliases = {0: 0, 1: 1}
    out_spec = pl.BlockSpec((tr, d_b), lambda i: (off + i, 0))
    return pl.pallas_call(
        functools.partial(_conv_kernel, tr=tr, seq_len=seq_len, mode=mode),
        grid=(rows // tr,),
        in_specs=in_specs,
        out_specs=[out_spec, out_spec],
        out_shape=[jax.ShapeDtypeStruct((n, d_b), BF16), jax.ShapeDtypeStruct((n, d_b), F32)],
        input_output_aliases=aliases,
        compiler_params=_cp(("arbitrary",)),
        name="conv_" + mode,
    )(*args)


def _attn_kernel(*refs, bb, tq, heads):
    q_ref, k_ref, v_ref, o_ref = refs[-4:]
    dh = k_ref.shape[2] // heads
    scale = dh ** -0.5
    row_batch = lax.broadcasted_iota(jnp.int32, (bb * tq, dh), 0) // tq
    for h in range(heads):
        cols = slice(h * dh, (h + 1) * dh)
        q = q_ref[:, cols].astype(BF16)
        out = None
        for b in range(bb):
            k = k_ref[b, :, cols].astype(BF16)
            v = v_ref[b, :, cols].astype(BF16)
            s = lax.dot_general(q, k, (((1,), (1,)), ((), ())), preferred_element_type=F32) * scale
            p = jnp.exp(s - jnp.max(s, axis=-1, keepdims=True))
            p = p / jnp.sum(p, axis=-1, keepdims=True)
            o = jnp.dot(p.astype(BF16), v, preferred_element_type=F32)
            out = o if out is None else jnp.where(row_batch == b, o, out)
        o_ref[:, cols] = out.astype(o_ref.dtype)


def _attention(q_src, q_col, row0, b, t, mem_k, mem_v, heads, out_rows, prev_out):
    n_mem, d_c = mem_k.shape[1:]
    if t >= 128:
        bb, tq = 1, _tile(t, 512, 16)
    else:
        kv_block_cap = max(1, ATTN_KV_BLOCK_BYTES // (n_mem * d_c * 4))
        bb, tq = _tile(b, min(kv_block_cap, max(1, 64 // t)), 1), t
    assert (bb * tq) % 16 == 0 and row0 % (bb * tq) == 0
    tpb = t // tq
    off = row0 // (bb * tq)
    qmap = lambda i, j: (off + i * tpb + j, q_col)
    omap = lambda i, j: (off + i * tpb + j, 0)
    kv = pl.BlockSpec((bb, n_mem, d_c), lambda i, j: (i, 0, 0))
    in_specs = [pl.BlockSpec((bb * tq, d_c), qmap), kv, kv]
    args, aliases = [q_src, mem_k, mem_v], {}
    if prev_out is not None:
        in_specs = [pl.BlockSpec(memory_space=pl.ANY)] + in_specs
        args, aliases = [prev_out] + args, {0: 0}
    return pl.pallas_call(
        functools.partial(_attn_kernel, bb=bb, tq=tq, heads=heads),
        grid=(b // bb, tpb),
        in_specs=in_specs,
        out_specs=pl.BlockSpec((bb * tq, d_c), omap),
        out_shape=jax.ShapeDtypeStruct((out_rows, d_c), BF16),
        input_output_aliases=aliases,
        compiler_params=_cp(("arbitrary", "arbitrary")),
        name="mem_attention",
    )(*args)


def _merge_kernel(ya_ref, yb_ref, yc_ref, ga_ref, gb_ref, gc_ref, wa_ref, wb_ref, wc_ref, o_ref):
    def branch(y_ref, g_ref, w_ref):
        return g_ref[...].astype(F32) * jnp.dot(
            y_ref[...], w_ref[...].astype(BF16), preferred_element_type=F32)

    acc = branch(ya_ref, ga_ref, wa_ref) + branch(yb_ref, gb_ref, wb_ref)
    o_ref[...] = (acc + branch(yc_ref, gc_ref, wc_ref)).astype(o_ref.dtype)


def _merge(ya, yb, yc, gates, wa, wb, wc):
    n, d = ya.shape[0], wa.shape[1]
    tm = _tile(n, 1100)
    tn = _tile(d, 512, V7X_LANES)
    nj = d // tn
    act = lambda y: pl.BlockSpec((tm, y.shape[1]), lambda i, j: (i, 0))
    gate = lambda c: pl.BlockSpec((tm, tn), lambda i, j: (i, c * nj + j))
    wsp = lambda w: pl.BlockSpec((w.shape[0], tn), lambda i, j: (0, j))
    return pl.pallas_call(
        _merge_kernel,
        grid=(n // tm, nj),
        in_specs=[act(ya), act(yb), act(yc), gate(0), gate(1), gate(2), wsp(wa), wsp(wb), wsp(wc)],
        out_specs=pl.BlockSpec((tm, tn), lambda i, j: (i, j)),
        out_shape=jax.ShapeDtypeStruct((n, d), BF16),
        compiler_params=_cp(("arbitrary", "arbitrary")),
        name="branch_merge",
    )(ya, yb, yc, gates, gates, gates, wa, wb, wc)


def _pad_rows(w, row0, total):
    return jnp.zeros((total, w.shape[1]), BF16).at[row0:row0 + w.shape[0]].set(w.astype(BF16))


def _first_rows(state, seq_len, width):
    b, k, c = state.shape
    out = jnp.zeros((b, seq_len, width), F32).at[:, :k, :c].set(state.astype(F32))
    return out.reshape(b * seq_len, width)


def _layer(x, h1, groups, lp, mem_kv):
    n, d = x.shape
    d_a = lp["w0"].shape[0]
    d_b = lp["conv_w"].shape[1]
    p_a_w = lp["mu_shift"].shape[0]
    lora_w, lora_a, lora_g = (lp[q].shape[0] for q in ("w_lora_up", "a_lora_up", "g_lora_up"))
    heads_c = mem_kv[0][0].shape[2]
    d_c = heads_c * mem_kv[0][0].shape[3]
    assert d_c == d_b and p_a_w == 3 * d_a + lora_w + lora_a + lora_g

    x = _matmul(_gateup(h1, lp["w_ffn1_in"]), lp["w_ffn1_out"], residual=x, tn=256, x_resident=True)
    h = _norm(x, lp["g_mix"])

    w_in = lp["w_in"]
    p_a = _matmul(h, w_in, col0=0, cols=p_a_w)
    project = _matmul_shifted if p_a_w % V7X_LANES else _matmul
    p_bc = project(h, w_in, col0=p_a_w, cols=3 * d_b + d_c)
    gates = project(h, w_in, col0=p_a_w + 3 * d_b + d_c, cols=3 * d, out_dtype=BF16, act="sigmoid")

    pw = p_a.shape[1]
    low_w = pw - 3 * d_a
    pre_params = (
        jnp.pad(lp["mu_shift"], (0, pw - p_a_w)).reshape(1, pw),
        lp["w0"].reshape(1, d_a), lp["a0"].reshape(1, d_a),
        _pad_rows(lp["w_lora_up"], 0, low_w),
        _pad_rows(lp["a_lora_up"], lora_w, low_w),
        _pad_rows(lp["g_lora_up"], lora_w + lora_a, low_w),
    )
    scan_params = (lp["k_k"], lp["k_a"], lp["r_k"].reshape(-1), lp["lnx_w"], lp["lnx_b"])
    ya, new_states = None, []
    for row0, b, t, st in groups:
        first = None if st is None else _first_rows(st["shift"][:, None, :], t, pw)
        *rwkva, g = _rwkv_pre(p_a, row0, b * t, t, first, pre_params, p_a_w - 3 * d_a)
        y, s_new = _rwkv_group(rwkva, b, t, None if st is None else st["rwkv"], scan_params)
        ya = _gate_rows(y, g, row0, n, ya)
        shift_new = p_a[row0 + t - 1:row0 + b * t:t, :p_a_w]
        new_states.append({"rwkv": s_new, "shift": shift_new})

    conv = None
    for row0, b, t, st in groups:
        firsts = None
        if st is not None:
            buf = st["conv"]
            firsts = (_first_rows(buf[:, 1:, :], t, d_b), _first_rows(buf, t, d_b))
        conv = _conv(p_bc, row0, b * t, t, firsts, lp["conv_w"], conv)
    yb, u2 = conv
    for (row0, b, t, st), ns in zip(groups, new_states):
        tail = [u2[row0 + t - q:row0 + b * t:t] for q in range(CONV_W - 1, 0, -1)]
        ns["conv"] = jnp.stack(tail, axis=1)

    yc = None
    for (row0, b, t, st), (mk, mv) in zip(groups, mem_kv):
        n_mem = mk.shape[1]
        yc = _attention(p_bc, (3 * d_b) // d_c, row0, b, t,
                        mk.reshape(b, n_mem, d_c), mv.reshape(b, n_mem, d_c), heads_c, n, yc)

    merged = _merge(ya, yb, yc, gates, lp["w_br_a"], lp["w_br_b"], lp["w_br_c"])
    x = _matmul(merged, lp["w_out"], residual=x)

    h = _norm(x, lp["g_ffn2"])
    x = _matmul(_gateup(h, lp["w_ffn2_in"]), lp["w_ffn2_out"], residual=x, tn=256, x_resident=True)
    return x, new_states


def kernel(x_prompt, x_sample, mem_prompt, cache_mem_k, cache_mem_v, state_rwkv, state_shift, state_conv, g_ffn1, w_ffn1_in, w_ffn1_out, g_mix, w_in, mu_shift, w0, w_lora_up, a0, a_lora_up, g_lora_up, k_k, k_a, r_k, lnx_w, lnx_b, conv_w, g_mem, w_mem_kv, w_br_a, w_br_b, w_br_c, w_out, g_ffn2, w_ffn2_in, w_ffn2_out, g_final):
    bp, tp, d = x_prompt.shape
    bs, ts, _ = x_sample.shape
    n_p, n_s = bp * tp, bs * ts
    depth = w_in.shape[0]
    n_mem, heads_c, dh_c = cache_mem_k.shape[2:]
    d_c = heads_c * dh_c

    per_layer = dict(
        g_ffn1=g_ffn1, w_ffn1_in=w_ffn1_in, w_ffn1_out=w_ffn1_out, g_mix=g_mix, w_in=w_in,
        mu_shift=mu_shift, w0=w0, w_lora_up=w_lora_up, a0=a0, a_lora_up=a_lora_up,
        g_lora_up=g_lora_up, k_k=k_k, k_a=k_a, r_k=r_k, lnx_w=lnx_w, lnx_b=lnx_b, conv_w=conv_w,
        w_br_a=w_br_a, w_br_b=w_br_b, w_br_c=w_br_c, w_out=w_out, g_ffn2=g_ffn2,
        w_ffn2_in=w_ffn2_in, w_ffn2_out=w_ffn2_out)

    mem_rows = mem_prompt.reshape(bp * n_mem, d)
    outs = {q: [] for q in ("mk", "mv", "rp", "shp", "cp", "rs", "shs", "cs")}
    x = None
    for l in range(depth):
        lp = {q: p[l] for q, p in per_layer.items()}
        if l == 0:
            x, h1 = _norm_concat(x_prompt.reshape(n_p, d), x_sample.reshape(n_s, d), lp["g_ffn1"])
        else:
            h1 = _norm(x, lp["g_ffn1"])
        kv = _matmul(_norm(mem_rows, g_mem[l]), w_mem_kv[l])
        mk_p = kv[:, :d_c].reshape(bp, n_mem, heads_c, dh_c)
        mv_p = kv[:, d_c:2 * d_c].reshape(bp, n_mem, heads_c, dh_c)
        groups = [
            (0, bp, tp, None),
            (n_p, bs, ts, {"rwkv": state_rwkv[l], "shift": state_shift[l], "conv": state_conv[l]}),
        ]
        x, (st_p, st_s) = _layer(
            x, h1, groups, lp, [(mk_p, mv_p), (cache_mem_k[l], cache_mem_v[l])])
        outs["mk"].append(mk_p)
        outs["mv"].append(mv_p)
        for tag, st in (("p", st_p), ("s", st_s)):
            outs["r" + tag].append(st["rwkv"])
            outs["sh" + tag].append(st["shift"])
            outs["c" + tag].append(st["conv"])
    y_prompt = _norm(x, g_final, F32, 0, n_p).reshape(bp, tp, d)
    y_sample = _norm(x, g_final, F32, n_p, n_s).reshape(bs, ts, d)
    stack = lambda q: jnp.stack(outs[q], 0)
    return (y_prompt, y_sample, stack("mk"), stack("mv"), stack("rp"), stack("shp"), stack("cp"),
            stack("rs"), stack("shs"), stack("cs"))
```

```python
import functools

import jax
import jax.numpy as jnp
from jax import lax
from jax.experimental import pallas as pl
from jax.experimental.pallas import tpu as pltpu

F32 = jnp.float32
BF16 = jnp.bfloat16

EPS = 1e-6
GN_EPS = 64e-5
HEAD_A = 64
CONV_W = 3
V7X_LANES = 128
V7X_SUBLANES = 8
V7X_VMEM_LIMIT = 56 * 1024 * 1024
W_LOG_SHIFT = 0.5
ATTN_KV_BLOCK_BYTES = 4 * 1024 * 1024
SCAN_J_BLOCK = 8


def _cp(sem, vmem=V7X_VMEM_LIMIT):
    return pltpu.CompilerParams(dimension_semantics=sem, vmem_limit_bytes=vmem)


def _tile(n, cap, mult=16):
    best = None
    for t in range(mult, min(n, cap) + 1, mult):
        if n % t == 0:
            best = t
    return best if best is not None else n


def _round_up(n, m):
    return (n + m - 1) // m * m


def _rms(x, g):
    ms = jnp.mean(x * x, axis=-1, keepdims=True)
    return (x * lax.rsqrt(ms + EPS)) * g


def _norm2_kernel(xp_ref, xs_ref, g_ref, x_ref, h_ref, *, n_prompt_tiles):
    i = pl.program_id(0)

    def emit(x):
        x_ref[...] = x
        h_ref[...] = _rms(x, g_ref[...]).astype(BF16)

    @pl.when(i < n_prompt_tiles)
    def _():
        emit(xp_ref[...])

    @pl.when(i >= n_prompt_tiles)
    def _():
        emit(xs_ref[...])


def _norm_concat(xp, xs, g):
    (n_p, d), n_s = xp.shape, xs.shape[0]
    tr = _tile(_gcd(n_p, n_s), 256, 8)
    npt, nst = n_p // tr, n_s // tr
    n = n_p + n_s
    return pl.pallas_call(
        functools.partial(_norm2_kernel, n_prompt_tiles=npt),
        grid=(npt + nst,),
        in_specs=[
            pl.BlockSpec((tr, d), lambda i: (jnp.minimum(i, npt - 1), 0)),
            pl.BlockSpec((tr, d), lambda i: (jnp.maximum(i - npt, 0), 0)),
            pl.BlockSpec((1, d), lambda i: (0, 0)),
        ],
        out_specs=[pl.BlockSpec((tr, d), lambda i: (i, 0)),
                   pl.BlockSpec((tr, d), lambda i: (i, 0))],
        out_shape=[jax.ShapeDtypeStruct((n, d), F32), jax.ShapeDtypeStruct((n, d), BF16)],
        compiler_params=_cp(("arbitrary",)),
        name="norm_concat",
    )(xp, xs, g.reshape(1, d))


def _gcd(a, b):
    while b:
        a, b = b, a % b
    return a


def _lcm(a, b):
    return a * b // _gcd(a, b)


def _norm_kernel(x_ref, g_ref, h_ref):
    h_ref[...] = _rms(x_ref[...], g_ref[...]).astype(h_ref.dtype)


def _norm(x, g, out_dtype=BF16, row0=0, rows=None):
    n, d = x.shape
    rows = n if rows is None else rows
    tr = _tile(_gcd(row0, rows) if row0 else rows, 256, 16)
    off = row0 // tr
    return pl.pallas_call(
        _norm_kernel,
        grid=(rows // tr,),
        in_specs=[pl.BlockSpec((tr, d), lambda i: (off + i, 0)),
                  pl.BlockSpec((1, d), lambda i: (0, 0))],
        out_specs=pl.BlockSpec((tr, d), lambda i: (i, 0)),
        out_shape=jax.ShapeDtypeStruct((rows, d), out_dtype),
        compiler_params=_cp(("arbitrary",)),
        name="norm",
    )(x, g.reshape(1, d))


def _gateup_kernel(h_ref, wg_ref, wu_ref, o_ref):
    h = h_ref[...]
    gate = jnp.dot(h, wg_ref[...].astype(BF16), preferred_element_type=F32)
    up = jnp.dot(h, wu_ref[...].astype(BF16), preferred_element_type=F32)
    o_ref[...] = ((gate * jax.nn.sigmoid(gate)) * up * 0.5).astype(o_ref.dtype)


def _gateup(h, w_in):
    n, d = h.shape
    f = w_in.shape[1] // 2
    tm = _tile(n, 1100)
    tn = _tile(f, 256, V7X_LANES)
    nf = f // tn
    return pl.pallas_call(
        _gateup_kernel,
        grid=(n // tm, nf),
        in_specs=[
            pl.BlockSpec((tm, d), lambda i, j: (i, 0)),
            pl.BlockSpec((d, tn), lambda i, j: (0, j)),
            pl.BlockSpec((d, tn), lambda i, j: (0, nf + j)),
        ],
        out_specs=pl.BlockSpec((tm, tn), lambda i, j: (i, j)),
        out_shape=jax.ShapeDtypeStruct((n, f), BF16),
        compiler_params=_cp(("arbitrary", "arbitrary")),
        name="ffn_gateup",
    )(h, w_in, w_in)


def _mm_kernel(x_ref, w_ref, o_ref, *, act):
    acc = jnp.dot(x_ref[...], w_ref[...].astype(BF16), preferred_element_type=F32)
    if act == "sigmoid":
        acc = jax.nn.sigmoid(acc)
    o_ref[...] = acc.astype(o_ref.dtype)


def _mm_res_kernel(x_ref, w_ref, r_ref, o_ref):
    o_ref[...] = r_ref[...] + jnp.dot(x_ref[...], w_ref[...].astype(BF16), preferred_element_type=F32)


_NT_DIMS = (((1,), (1,)), ((), ()))


def _mm_nt_kernel(x_ref, w_ref, o_ref, *, act):
    acc = lax.dot_general(x_ref[...], w_ref[...].astype(BF16), _NT_DIMS, preferred_element_type=F32)
    if act == "sigmoid":
        acc = jax.nn.sigmoid(acc)
    o_ref[...] = acc.astype(o_ref.dtype)


def _mm_nt_shift_kernel(x_ref, wa_ref, wb_ref, o_ref, w_scr, *, act):
    @pl.when(pl.program_id(1) == 0)
    def _():
        head = wa_ref.shape[0]
        w_scr[:head] = wa_ref[...].astype(BF16)
        w_scr[head:] = wb_ref[:w_scr.shape[0] - head].astype(BF16)

    acc = lax.dot_general(x_ref[...], w_scr[...], _NT_DIMS, preferred_element_type=F32)
    if act == "sigmoid":
        acc = jax.nn.sigmoid(acc)
    o_ref[...] = acc.astype(o_ref.dtype)


def _matmul_nt(x, wt, *, row0=0, rows=None, out_dtype=F32, act=None, tn=512):
    m, k = x.shape
    rows = wt.shape[0] - row0 if rows is None else rows
    tn = min(tn, _round_up(rows, V7X_LANES))
    nb = -(-rows // tn)
    tm = _tile(m, 1100)
    out_shape = jax.ShapeDtypeStruct((m, nb * tn), out_dtype)
    if row0 % tn == 0:
        jb = row0 // tn
        return pl.pallas_call(
            functools.partial(_mm_nt_kernel, act=act),
            grid=(m // tm, nb),
            in_specs=[pl.BlockSpec((tm, k), lambda i, j: (i, 0)),
                      pl.BlockSpec((tn, k), lambda i, j: (jb + j, 0))],
            out_specs=pl.BlockSpec((tm, tn), lambda i, j: (i, j)),
            out_shape=out_shape,
            compiler_params=_cp(("arbitrary", "arbitrary")),
            name="matmul_nt",
        )(x, wt)
    head = tn - row0 % tn
    assert head % 16 == 0 and row0 % head == 0 and tn % head == 0
    ja, jb, step = row0 // head, (row0 + head) // tn, tn // head
    return pl.pallas_call(
        functools.partial(_mm_nt_shift_kernel, act=act),
        grid=(nb, m // tm),
        in_specs=[pl.BlockSpec((tm, k), lambda j, i: (i, 0)),
                  pl.BlockSpec((head, k), lambda j, i: (ja + step * j, 0)),
                  pl.BlockSpec((tn, k), lambda j, i: (jb + j, 0))],
        out_specs=pl.BlockSpec((tm, tn), lambda j, i: (i, j)),
        out_shape=out_shape,
        scratch_shapes=[pltpu.VMEM((tn, k), BF16)],
        compiler_params=_cp(("arbitrary", "arbitrary")),
        name="matmul_nt_shifted",
    )(x, wt, wt)


def _matmul(x, w, *, col0=0, cols=None, out_dtype=F32, act=None, residual=None, tn=512,
            x_resident=False):
    m, k = x.shape
    cols = w.shape[1] - col0 if cols is None else cols
    tn = min(tn, _round_up(cols, V7X_LANES))
    assert col0 % tn == 0
    nb, jb = -(-cols // tn), col0 // tn
    tm = _tile(m, 1100)
    x_mode = pl.Buffered(1) if x_resident else None
    in_specs = [pl.BlockSpec((tm, k), lambda i, j: (i, 0), pipeline_mode=x_mode),
                pl.BlockSpec((k, tn), lambda i, j: (0, jb + j))]
    args = [x, w]
    if residual is None:
        kern = functools.partial(_mm_kernel, act=act)
    else:
        kern = _mm_res_kernel
        in_specs.append(pl.BlockSpec((tm, tn), lambda i, j: (i, j)))
        args.append(residual)
    return pl.pallas_call(
        kern,
        grid=(m // tm, nb),
        in_specs=in_specs,
        out_specs=pl.BlockSpec((tm, tn), lambda i, j: (i, j)),
        out_shape=jax.ShapeDtypeStruct((m, nb * tn), out_dtype),
        compiler_params=_cp(("arbitrary", "arbitrary")),
        name="matmul",
    )(*args)


def _shift_rows(x, k, tile_row0, seq_len, halo):
    rows = lax.broadcasted_iota(jnp.int32, x.shape, 0)
    out = pltpu.roll(x, k, axis=0)
    at_start = (tile_row0 % seq_len) == 0
    for q in range(k):
        h = halo[V7X_SUBLANES - k + q:V7X_SUBLANES - k + q + 1, :]
        h = jnp.where(at_start, jnp.zeros_like(h), h)
        out = jnp.where(rows == q, h, out)
    return out


def _premix(x, prev, param_refs, d_a, low_valid):
    mu_ref, w0_ref, a0_ref, lw_ref, la_ref, lg_ref = param_refs
    ps = x + (prev - x) * mu_ref[...]
    low = ps[:, 3 * d_a:]
    low = jnp.where(lax.broadcasted_iota(jnp.int32, low.shape, 1) < low_valid, low, 0.0)
    z = w0_ref[...] + jnp.dot(jnp.tanh(low).astype(BF16), lw_ref[...], preferred_element_type=F32)
    decay = jnp.exp(-jnp.exp(-W_LOG_SHIFT) * jax.nn.sigmoid(z))
    a = jax.nn.sigmoid(a0_ref[...] + jnp.dot(low.astype(BF16), la_ref[...], preferred_element_type=F32))
    g = jnp.dot(jax.nn.sigmoid(low).astype(BF16), lg_ref[...], preferred_element_type=F32)
    return ps[:, :d_a], decay, ps[:, d_a:2 * d_a], ps[:, 2 * d_a:3 * d_a], a, g


def _rwkv_pre_first_kernel(x_ref, first_ref, *refs, d_a, low_valid, seq_len):
    x = x_ref[...]
    t = lax.broadcasted_iota(jnp.int32, x.shape, 0) % seq_len
    prev = jnp.where(t == 0, first_ref[...], pltpu.roll(x, 1, axis=0))
    for o_ref, val in zip(refs[6:], _premix(x, prev, refs[:6], d_a, low_valid)):
        o_ref[...] = val


def _rwkv_pre_first(p_a, row0, rows, seq_len, first, params, low_valid):
    pw = p_a.shape[1]
    d_a = params[1].shape[1]
    tr = _tile(rows, 128, _lcm(seq_len, V7X_SUBLANES))
    assert row0 % tr == 0 and rows % tr == 0
    off = row0 // tr
    full = lambda a: pl.BlockSpec(a.shape, lambda i: (0, 0))
    out_spec = pl.BlockSpec((tr, d_a), lambda i: (i, 0))
    return pl.pallas_call(
        functools.partial(_rwkv_pre_first_kernel, d_a=d_a, low_valid=low_valid, seq_len=seq_len),
        grid=(rows // tr,),
        in_specs=[pl.BlockSpec((tr, pw), lambda i: (off + i, 0)),
                  pl.BlockSpec((tr, pw), lambda i: (i, 0))] + [full(p) for p in params],
        out_specs=[out_spec] * 6,
        out_shape=[jax.ShapeDtypeStruct((rows, d_a), F32)] * 6,
        compiler_params=_cp(("arbitrary",)),
        name="rwkv_pre_first",
    )(p_a, first, *params)


def _chans_to_lanes(xs, n_heads):
    bg = len(xs)
    ln = bg * n_heads
    group = lax.broadcasted_iota(jnp.int32, (xs[0].shape[0], ln), 1) // n_heads
    out = []
    for q in range(HEAD_A // bg):
        chunk = [x[:, q * ln:(q + 1) * ln] for x in xs]
        rolled = [[c if s == 0 else pltpu.roll(c, s * n_heads, axis=1) for s in range(bg)] for c in chunk]
        for jl in range(bg):
            acc = rolled[bg - 1][(bg - 1 - jl) % bg]
            for b in range(bg - 2, -1, -1):
                acc = jnp.where(group == b, rolled[b][(b - jl) % bg], acc)
            out.append(acc)
    return out


def _rwkv_pre_lanes_kernel(*refs, bg, d_a, low_valid, tr, seq_len):
    x_refs, halo_refs = refs[:bg], refs[bg:2 * bg]
    param_refs = refs[2 * bg:2 * bg + 6]
    r_ref, w_ref, k_ref, a_ref, v_ref, g_ref = refs[2 * bg + 6:]
    n_heads = d_a // HEAD_A
    mixed = []
    for b in range(bg):
        x = x_refs[b][...]
        prev = _shift_rows(x, 1, pl.program_id(1) * tr, seq_len, halo_refs[b][...])
        r, w, k, v, a, g = _premix(x, prev, param_refs, d_a, low_valid)
        v_ref[b] = v
        g_ref[b] = g
        mixed.append((r, w, k, a))
    for o_ref, vals in zip((r_ref, w_ref, k_ref, a_ref), zip(*mixed)):
        for j, val in enumerate(_chans_to_lanes(vals, n_heads)):
            o_ref[0, j] = val


def _rwkv_pre_lanes(p_a, row0, batch, seq_len, params, low_valid, bg):
    pw = p_a.shape[1]
    d_a = params[1].shape[1]
    ln = bg * (d_a // HEAD_A)
    tr = _tile(seq_len, 32, V7X_SUBLANES)
    assert row0 % tr == 0 and batch % bg == 0
    off, tpb, hb = row0 // tr, seq_len // tr, tr // V7X_SUBLANES
    xs = [pl.BlockSpec((tr, pw), lambda g, i, b=b: (off + (g * bg + b) * tpb + i, 0)) for b in range(bg)]
    halos = [pl.BlockSpec((V7X_SUBLANES, pw),
                          lambda g, i, b=b: (jnp.maximum((off + (g * bg + b) * tpb + i) * hb - 1, 0), 0))
             for b in range(bg)]
    full = lambda a: pl.BlockSpec(a.shape, lambda g, i: (0, 0))
    lanes = pl.BlockSpec((1, HEAD_A, tr, ln), lambda g, i: (g, 0, i, 0))
    rowsp = pl.BlockSpec((bg, tr, d_a), lambda g, i: (g, i, 0))
    lanes_shape = jax.ShapeDtypeStruct((batch // bg, HEAD_A, seq_len, ln), F32)
    rows_shape = jax.ShapeDtypeStruct((batch, seq_len, d_a), F32)
    r, w, k, a, v, g = pl.pallas_call(
        functools.partial(_rwkv_pre_lanes_kernel, bg=bg, d_a=d_a, low_valid=low_valid, tr=tr,
                          seq_len=seq_len),
        grid=(batch // bg, tpb),
        in_specs=xs + halos + [full(p) for p in params],
        out_specs=[lanes] * 4 + [rowsp] * 2,
        out_shape=[lanes_shape] * 4 + [rows_shape] * 2,
        compiler_params=_cp(("arbitrary", "arbitrary")),
        name="rwkv_pre_lanes",
    )(*([p_a] * (2 * bg)), *params)
    return r, w, k, a, v.reshape(batch * seq_len, d_a), g.reshape(batch * seq_len, d_a)


def _scan_kernel(r_ref, w_ref, k_ref, a_ref, v_ref, kk_ref, ka_ref, rk_ref, lnw_ref, lnb_ref,
                 s0_ref, y_ref, s_ref, av_ref, bv_ref, km_ref, bonus_ref, *, tc):
    hd = s_ref.shape[1]
    row = lambda ref, j: ref[pl.ds(j, 1), :]

    @pl.when(pl.program_id(1) == 0)
    def _():
        s_ref[...] = s0_ref[...]

    ss = jnp.zeros(bonus_ref.shape, F32)
    for j in range(hd):
        kk = k_ref[0, j] * row(kk_ref, j)
        ss = ss + kk * kk
    inv = 1.0 / jnp.maximum(jnp.sqrt(ss), 1e-12)
    bonus = jnp.zeros(bonus_ref.shape, F32)
    for j in range(hd):
        k, a = k_ref[0, j], a_ref[0, j]
        kk = (k * row(kk_ref, j)) * inv
        kmod = k * (1.0 + (a - 1.0) * row(ka_ref, j))
        av_ref[j] = -kk
        bv_ref[j] = kk * a
        km_ref[j] = kmod
        bonus = bonus + (r_ref[0, j] * kmod) * row(rk_ref, j)
    bonus_ref[...] = bonus

    def step(t, carry):
        at = pl.ds(t, 1)
        v = v_ref[0, t]

        def project(jb, sa):
            for jj in range(SCAN_J_BLOCK):
                j = jb * SCAN_J_BLOCK + jj
                sa = sa + s_ref[0, j] * av_ref[j, at, :]
            return sa

        sa = lax.fori_loop(0, hd // SCAN_J_BLOCK, project, jnp.zeros_like(v))

        def update(jb, y):
            for jj in range(SCAN_J_BLOCK):
                j = jb * SCAN_J_BLOCK + jj
                sj = (s_ref[0, j] * w_ref[0, j, at, :] + sa * bv_ref[j, at, :]) + v * km_ref[j, at, :]
                s_ref[0, j] = sj
                y = y + sj * r_ref[0, j, at, :]
            return y

        y = lax.fori_loop(0, hd // SCAN_J_BLOCK, update, jnp.zeros_like(v))
        mu = jnp.mean(y, axis=0, keepdims=True)
        yc = y - mu
        var = jnp.mean(yc * yc, axis=0, keepdims=True)
        yn = (yc * lax.rsqrt(var + GN_EPS)) * lnw_ref[...] + lnb_ref[...]
        y_ref[0, t] = yn + bonus_ref[at, :] * v
        return carry

    lax.fori_loop(0, tc, step, 0)


def _scan(r, w, k, a, v, params, s0):
    g, hd, t, ln = r.shape
    tc = _tile(t, 64, 1)
    by_chan = pl.BlockSpec((1, hd, tc, ln), lambda gi, c: (gi, 0, c, 0))
    by_time = pl.BlockSpec((1, tc, hd, ln), lambda gi, c: (gi, c, 0, 0))
    par = pl.BlockSpec((hd, ln), lambda gi, c: (0, 0))
    st = pl.BlockSpec((1, hd, hd, ln), lambda gi, c: (gi, 0, 0, 0))
    return pl.pallas_call(
        functools.partial(_scan_kernel, tc=tc),
        grid=(g, t // tc),
        in_specs=[by_chan] * 4 + [by_time] + [par] * 5 + [st],
        out_specs=[by_time, st],
        out_shape=[jax.ShapeDtypeStruct(v.shape, F32), jax.ShapeDtypeStruct(s0.shape, F32)],
        scratch_shapes=[pltpu.VMEM((hd, tc, ln), F32)] * 3 + [pltpu.VMEM((tc, ln), F32)],
        compiler_params=_cp(("arbitrary", "arbitrary")),
        name="rwkv_scan",
    )(r, w, k, a, v, *params, s0)


def _to_heads_on_lanes(x2d, b, t, bg):
    h = x2d.shape[1] // HEAD_A
    x = x2d.reshape(b // bg, bg, t, h, HEAD_A)
    return x.transpose(0, 2, 4, 1, 3).reshape(b // bg, t, HEAD_A, bg * h)


def _to_chan_major_lanes(x2d, b, t, bg):
    h = x2d.shape[1] // HEAD_A
    x = x2d.reshape(b // bg, bg, t, HEAD_A, h)
    return x.transpose(0, 3, 2, 1, 4).reshape(b // bg, HEAD_A, t, bg * h)


def _from_heads_on_lanes(x, b, t, bg):
    h = x.shape[-1] // bg
    x = x.reshape(b // bg, t, HEAD_A, bg, h)
    return x.transpose(0, 3, 1, 4, 2).reshape(b * t, h * HEAD_A)


def _param_on_lanes(p, bg):
    h = p.size // HEAD_A
    return jnp.tile(p.reshape(h, HEAD_A).T, (1, bg))


def _lane_batch(b, n_heads):
    return max(1, min(b, V7X_LANES // n_heads))


def _rwkv_group(rwka, v, b, t, s_prev, scan_params):
    h = v.shape[1] // HEAD_A
    bg = _lane_batch(b, h)
    seqs = list(rwka) + [_to_heads_on_lanes(v, b, t, bg)]
    params = [_param_on_lanes(p, bg) for p in scan_params]
    if s_prev is None:
        s0 = jnp.zeros((b // bg, HEAD_A, HEAD_A, bg * h), F32)
    else:
        s0 = s_prev.astype(F32).reshape(b // bg, bg, h, HEAD_A, HEAD_A)
        s0 = s0.transpose(0, 4, 3, 1, 2).reshape(b // bg, HEAD_A, HEAD_A, bg * h)
    y, s_new = _scan(*seqs, params, s0)
    s_new = s_new.reshape(b // bg, HEAD_A, HEAD_A, bg, h).transpose(0, 3, 4, 2, 1)
    return _from_heads_on_lanes(y, b, t, bg), s_new.reshape(b, h, HEAD_A, HEAD_A)


def _gate_kernel(*refs):
    y_ref, g_ref, o_ref = refs[-3:]
    o_ref[...] = (y_ref[...] * g_ref[...]).astype(o_ref.dtype)


def _gate_rows(y, g, row0, n, prev_out):
    rows, d_a = y.shape
    tr = _tile(_gcd(row0, rows) if row0 else rows, 512, 16)
    off = row0 // tr
    in_specs = [pl.BlockSpec((tr, d_a), lambda i: (i, 0)),
                pl.BlockSpec((tr, d_a), lambda i: (i, 0))]
    args, aliases = [y, g], {}
    if prev_out is not None:
        in_specs = [pl.BlockSpec(memory_space=pl.ANY)] + in_specs
        args, aliases = [prev_out] + args, {0: 0}
    return pl.pallas_call(
        _gate_kernel,
        grid=(rows // tr,),
        in_specs=in_specs,
        out_specs=pl.BlockSpec((tr, d_a), lambda i: (off + i, 0)),
        out_shape=jax.ShapeDtypeStruct((n, d_a), BF16),
        input_output_aliases=aliases,
        compiler_params=_cp(("arbitrary",)),
        name="rwkv_gate",
    )(*args)


def _conv_kernel(*refs, tr, seq_len, mode):
    gb_ref, gc_ref, u_ref, aux1_ref, aux2_ref, cw_ref = refs[-8:-2]
    y_ref, u2_ref = refs[-2:]
    f1_ref, f2_ref = aux1_ref, aux2_ref
    u2 = gc_ref[...] * u_ref[...]
    if mode == "halo":
        halo = aux1_ref[...] * aux2_ref[...]
        row0 = pl.program_id(0) * tr
        p1 = _shift_rows(u2, 1, row0, seq_len, halo)
        p2 = _shift_rows(u2, 2, row0, seq_len, halo)
    else:
        t = lax.broadcasted_iota(jnp.int32, u2.shape, 0) % seq_len
        p1 = jnp.where(t < 1, f1_ref[...], pltpu.roll(u2, 1, axis=0))
        p2 = jnp.where(t < 2, f2_ref[...], pltpu.roll(u2, 2, axis=0))
    z = p2 * cw_ref[0:1, :] + p1 * cw_ref[1:2, :]
    z = z + u2 * cw_ref[2:3, :]
    y_ref[...] = (gb_ref[...] * z).astype(y_ref.dtype)
    u2_ref[...] = u2


def _conv(p_bc, row0, rows, seq_len, firsts, conv_w, prev_out):
    n = p_bc.shape[0]
    d_b = conv_w.shape[1]
    mode = "halo" if firsts is None else "first"
    if mode == "halo":
        tr = _tile(_gcd(seq_len, rows), 256, 16)
    else:
        tr = _tile(rows, 256, _lcm(seq_len, 16))
    assert row0 % tr == 0 and rows % tr == 0 and seq_len >= CONV_W - 1
    off = row0 // tr
    col = lambda c: pl.BlockSpec((tr, d_b), lambda i: (off + i, c))
    in_specs = [col(0), col(1), col(2)]
    args = [p_bc, p_bc, p_bc]
    if mode == "halo":
        hb = tr // V7X_SUBLANES
        hcol = lambda c: pl.BlockSpec(
            (V7X_SUBLANES, d_b), lambda i: (jnp.maximum((off + i) * hb - 1, 0), c))
        in_specs += [hcol(1), hcol(2)]
        args += [p_bc, p_bc]
    else:
        in_specs += [pl.BlockSpec((tr, d_b), lambda i: (i, 0))] * 2
        args += list(firsts)
    in_specs.append(pl.BlockSpec(conv_w.shape, lambda i: (0, 0)))
    args.append(conv_w)
    aliases = {}
    if prev_out is not None:
        in_specs = [pl.BlockSpec(memory_space=pl.ANY)] * 2 + in_specs
        args = list(prev_out) + args
        aliases = {0: 0, 1: 1}
    out_spec = pl.BlockSpec((tr, d_b), lambda i: (off + i, 0))
    return pl.pallas_call(
        functools.partial(_conv_kernel, tr=tr, seq_len=seq_len, mode=mode),
        grid=(rows // tr,),
        in_specs=in_specs,
        out_specs=[out_spec, out_spec],
        out_shape=[jax.ShapeDtypeStruct((n, d_b), BF16), jax.ShapeDtypeStruct((n, d_b), F32)],
        input_output_aliases=aliases,
        compiler_params=_cp(("arbitrary",)),
        name="conv_" + mode,
    )(*args)


def _attn_kernel(*refs, bb, tq, heads):
    q_ref, k_ref, v_ref, o_ref = refs[-4:]
    dh = k_ref.shape[2] // heads
    scale = dh ** -0.5
    row_batch = lax.broadcasted_iota(jnp.int32, (bb * tq, dh), 0) // tq
    for h in range(heads):
        cols = slice(h * dh, (h + 1) * dh)
        q = q_ref[:, cols].astype(BF16)
        out = None
        for b in range(bb):
            k = k_ref[b, :, cols].astype(BF16)
            v = v_ref[b, :, cols].astype(BF16)
            s = lax.dot_general(q, k, (((1,), (1,)), ((), ())), preferred_element_type=F32) * scale
            p = jnp.exp(s - jnp.max(s, axis=-1, keepdims=True))
            p = p / jnp.sum(p, axis=-1, keepdims=True)
            o = jnp.dot(p.astype(BF16), v, preferred_element_type=F32)
            out = o if out is None else jnp.where(row_batch == b, o, out)
        o_ref[:, cols] = out.astype(o_ref.dtype)


def _attention(q_src, q_col, row0, b, t, mem_k, mem_v, heads, out_rows, prev_out):
    n_mem, d_c = mem_k.shape[1:]
    if t >= 128:
        bb, tq = 1, _tile(t, 512, 16)
    else:
        kv_block_cap = max(1, ATTN_KV_BLOCK_BYTES // (n_mem * d_c * 4))
        bb, tq = _tile(b, min(kv_block_cap, max(1, 64 // t)), 1), t
    assert (bb * tq) % 16 == 0 and row0 % (bb * tq) == 0
    tpb = t // tq
    off = row0 // (bb * tq)
    qmap = lambda i, j: (off + i * tpb + j, q_col)
    omap = lambda i, j: (off + i * tpb + j, 0)
    kv = pl.BlockSpec((bb, n_mem, d_c), lambda i, j: (i, 0, 0))
    in_specs = [pl.BlockSpec((bb * tq, d_c), qmap), kv, kv]
    args, aliases = [q_src, mem_k, mem_v], {}
    if prev_out is not None:
        in_specs = [pl.BlockSpec(memory_space=pl.ANY)] + in_specs
        args, aliases = [prev_out] + args, {0: 0}
    return pl.pallas_call(
        functools.partial(_attn_kernel, bb=bb, tq=tq, heads=heads),
        grid=(b // bb, tpb),
        in_specs=in_specs,
        out_specs=pl.BlockSpec((bb * tq, d_c), omap),
        out_shape=jax.ShapeDtypeStruct((out_rows, d_c), BF16),
        input_output_aliases=aliases,
        compiler_params=_cp(("arbitrary", "arbitrary")),
        name="mem_attention",
    )(*args)


def _merge_kernel(ya_ref, yb_ref, yc_ref, ga_ref, gb_ref, gc_ref, wa_ref, wb_ref, wc_ref, o_ref):
    def branch(y_ref, g_ref, w_ref):
        return g_ref[...].astype(F32) * jnp.dot(
            y_ref[...], w_ref[...].astype(BF16), preferred_element_type=F32)

    acc = branch(ya_ref, ga_ref, wa_ref) + branch(yb_ref, gb_ref, wb_ref)
    o_ref[...] = (acc + branch(yc_ref, gc_ref, wc_ref)).astype(o_ref.dtype)


def _merge(ya, yb, yc, gates, wa, wb, wc):
    n, d = ya.shape[0], wa.shape[1]
    tm = _tile(n, 1100)
    tn = _tile(d, 512, V7X_LANES)
    nj = d // tn
    act = lambda y: pl.BlockSpec((tm, y.shape[1]), lambda i, j: (i, 0))
    gate = lambda c: pl.BlockSpec((tm, tn), lambda i, j: (i, c * nj + j))
    wsp = lambda w: pl.BlockSpec((w.shape[0], tn), lambda i, j: (0, j))
    return pl.pallas_call(
        _merge_kernel,
        grid=(n // tm, nj),
        in_specs=[act(ya), act(yb), act(yc), gate(0), gate(1), gate(2), wsp(wa), wsp(wb), wsp(wc)],
        out_specs=pl.BlockSpec((tm, tn), lambda i, j: (i, j)),
        out_shape=jax.ShapeDtypeStruct((n, d), BF16),
        compiler_params=_cp(("arbitrary", "arbitrary")),
        name="branch_merge",
    )(ya, yb, yc, gates, gates, gates, wa, wb, wc)


def _swap_head_chan(p, n_heads, to_chan_major, axis=-1):
    axis = axis % p.ndim
    blocks = p.shape[axis] // (n_heads * HEAD_A)
    inner = (n_heads, HEAD_A) if to_chan_major else (HEAD_A, n_heads)
    q = p.reshape(p.shape[:axis] + (blocks,) + inner + p.shape[axis + 1:])
    return jnp.swapaxes(q, axis + 1, axis + 2).reshape(p.shape)


def _pad_rows(w, row0, total):
    return jnp.zeros((total, w.shape[1]), BF16).at[row0:row0 + w.shape[0]].set(w.astype(BF16))


def _first_rows(state, seq_len, width):
    b, k, c = state.shape
    out = jnp.zeros((b, seq_len, width), F32).at[:, :k, :c].set(state.astype(F32))
    return out.reshape(b * seq_len, width)


def _layer(x, h1, groups, lp, mem_kv):
    n, d = x.shape
    d_a = lp["w0"].shape[0]
    d_b = lp["conv_w"].shape[1]
    p_a_w = lp["mu_shift"].shape[0]
    lora_w, lora_a, lora_g = (lp[q].shape[0] for q in ("w_lora_up", "a_lora_up", "g_lora_up"))
    heads_c = mem_kv[0][0].shape[2]
    d_c = heads_c * mem_kv[0][0].shape[3]
    assert d_c == d_b and p_a_w == 3 * d_a + lora_w + lora_a + lora_g

    x = _matmul(_gateup(h1, lp["w_ffn1_in"]), lp["w_ffn1_out"], residual=x, tn=256, x_resident=True)
    h = _norm(x, lp["g_mix"])

    w_t = jnp.swapaxes(lp["w_in"], 0, 1)
    pw = _round_up(p_a_w, 512)
    w_a = jnp.concatenate([_swap_head_chan(w_t[:2 * d_a], d_a // HEAD_A, to_chan_major=True, axis=0),
                           w_t[2 * d_a:pw]], axis=0)
    p_a = _matmul_nt(h, w_a)
    p_bc = _matmul_nt(h, w_t, row0=p_a_w, rows=3 * d_b + d_c)
    gates = _matmul_nt(h, w_t, row0=p_a_w + 3 * d_b + d_c, rows=3 * d, out_dtype=BF16, act="sigmoid")

    pw = p_a.shape[1]
    low_w = pw - 3 * d_a
    n_heads = d_a // HEAD_A
    rk_swap = lambda p: jnp.concatenate(
        [_swap_head_chan(p[..., :2 * d_a], n_heads, to_chan_major=True), p[..., 2 * d_a:]], axis=-1)
    to_cm = lambda p: _swap_head_chan(p, n_heads, to_chan_major=True)
    pre_params = (
        jnp.pad(rk_swap(lp["mu_shift"]), (0, pw - p_a_w)).reshape(1, pw),
        to_cm(lp["w0"]).reshape(1, d_a), to_cm(lp["a0"]).reshape(1, d_a),
        _pad_rows(to_cm(lp["w_lora_up"]), 0, low_w),
        _pad_rows(to_cm(lp["a_lora_up"]), lora_w, low_w),
        _pad_rows(lp["g_lora_up"], lora_w + lora_a, low_w),
    )
    scan_params = (lp["k_k"], lp["k_a"], lp["r_k"].reshape(-1), lp["lnx_w"], lp["lnx_b"])
    ya, new_states = None, []
    for row0, b, t, st in groups:
        bg = _lane_batch(b, n_heads)
        if st is None:
            *rwka, v, g = _rwkv_pre_lanes(p_a, row0, b, t, pre_params, p_a_w - 3 * d_a, bg)
        else:
            first = _first_rows(rk_swap(st["shift"])[:, None, :], t, pw)
            r, w, k, v, a, g = _rwkv_pre_first(p_a, row0, b * t, t, first, pre_params, p_a_w - 3 * d_a)
            rwka = [_to_chan_major_lanes(q, b, t, bg) for q in (r, w, k, a)]
        y, s_new = _rwkv_group(rwka, v, b, t, None if st is None else st["rwkv"], scan_params)
        ya = _gate_rows(y, g, row0, n, ya)
        last = p_a[row0 + t - 1:row0 + b * t:t, :p_a_w]
        shift_new = jnp.concatenate(
            [_swap_head_chan(last[:, :2 * d_a], n_heads, to_chan_major=False), last[:, 2 * d_a:]], axis=1)
        new_states.append({"rwkv": s_new, "shift": shift_new})

    conv = None
    for row0, b, t, st in groups:
        firsts = None
        if st is not None:
            buf = st["conv"]
            firsts = (_first_rows(buf[:, 1:, :], t, d_b), _first_rows(buf, t, d_b))
        conv = _conv(p_bc, row0, b * t, t, firsts, lp["conv_w"], conv)
    yb, u2 = conv
    for (row0, b, t, st), ns in zip(groups, new_states):
        tail = [u2[row0 + t - q:row0 + b * t:t] for q in range(CONV_W - 1, 0, -1)]
        ns["conv"] = jnp.stack(tail, axis=1)

    yc = None
    for (row0, b, t, st), (mk, mv) in zip(groups, mem_kv):
        n_mem = mk.shape[1]
        yc = _attention(p_bc, (3 * d_b) // d_c, row0, b, t,
                        mk.reshape(b, n_mem, d_c), mv.reshape(b, n_mem, d_c), heads_c, n, yc)

    merged = _merge(ya, yb, yc, gates, lp["w_br_a"], lp["w_br_b"], lp["w_br_c"])
    x = _matmul(merged, lp["w_out"], residual=x)

    h = _norm(x, lp["g_ffn2"])
    x = _matmul(_gateup(h, lp["w_ffn2_in"]), lp["w_ffn2_out"], residual=x, tn=256, x_resident=True)
    return x, new_states


def kernel(x_prompt, x_sample, mem_prompt, cache_mem_k, cache_mem_v, state_rwkv, state_shift, state_conv, g_ffn1, w_ffn1_in, w_ffn1_out, g_mix, w_in, mu_shift, w0, w_lora_up, a0, a_lora_up, g_lora_up, k_k, k_a, r_k, lnx_w, lnx_b, conv_w, g_mem, w_mem_kv, w_br_a, w_br_b, w_br_c, w_out, g_ffn2, w_ffn2_in, w_ffn2_out, g_final):
    bp, tp, d = x_prompt.shape
    bs, ts, _ = x_sample.shape
    n_p, n_s = bp * tp, bs * ts
    depth = w_in.shape[0]
    n_mem, heads_c, dh_c = cache_mem_k.shape[2:]
    d_c = heads_c * dh_c

    per_layer = dict(
        g_ffn1=g_ffn1, w_ffn1_in=w_ffn1_in, w_ffn1_out=w_ffn1_out, g_mix=g_mix, w_in=w_in,
        mu_shift=mu_shift, w0=w0, w_lora_up=w_lora_up, a0=a0, a_lora_up=a_lora_up,
        g_lora_up=g_lora_up, k_k=k_k, k_a=k_a, r_k=r_k, lnx_w=lnx_w, lnx_b=lnx_b, conv_w=conv_w,
        w_br_a=w_br_a, w_br_b=w_br_b, w_br_c=w_br_c, w_out=w_out, g_ffn2=g_ffn2,
        w_ffn2_in=w_ffn2_in, w_ffn2_out=w_ffn2_out)

    mem_rows = mem_prompt.reshape(bp * n_mem, d)
    outs = {q: [] for q in ("mk", "mv", "rp", "shp", "cp", "rs", "shs", "cs")}
    x = None
    for l in range(depth):
        lp = {q: p[l] for q, p in per_layer.items()}
        if l == 0:
            x, h1 = _norm_concat(x_prompt.reshape(n_p, d), x_sample.reshape(n_s, d), lp["g_ffn1"])
        else:
            h1 = _norm(x, lp["g_ffn1"])
        kv = _matmul(_norm(mem_rows, g_mem[l]), w_mem_kv[l])
        mk_p = kv[:, :d_c].reshape(bp, n_mem, heads_c, dh_c)
        mv_p = kv[:, d_c:2 * d_c].reshape(bp, n_mem, heads_c, dh_c)
        groups = [
            (0, bp, tp, None),
            (n_p, bs, ts, {"rwkv": state_rwkv[l], "shift": state_shift[l], "conv": state_conv[l]}),
        ]
        x, (st_p, st_s) = _layer(
            x, h1, groups, lp, [(mk_p, mv_p), (cache_mem_k[l], cache_mem_v[l])])
        outs["mk"].append(mk_p)
        outs["mv"].append(mv_p)
        for tag, st in (("p", st_p), ("s", st_s)):
            outs["r" + tag].append(st["rwkv"])
            outs["sh" + tag].append(st["shift"])
            outs["c" + tag].append(st["conv"])
    y_prompt = _norm(x, g_final, F32, 0, n_p).reshape(bp, tp, d)
    y_sample = _norm(x, g_final, F32, n_p, n_s).reshape(bs, ts, d)
    stack = lambda q: jnp.stack(outs[q], 0)
    return (y_prompt, y_sample, stack("mk"), stack("mv"), stack("rp"), stack("shp"), stack("cp"),
            stack("rs"), stack("shs"), stack("cs"))
```

```python
import functools

import jax
import jax.numpy as jnp
from jax import lax
from jax.experimental import pallas as pl
from jax.experimental.pallas import tpu as pltpu

F32 = jnp.float32
BF16 = jnp.bfloat16

EPS = 1e-6
GN_EPS = 64e-5
HEAD_A = 64
CONV_W = 3
V7X_LANES = 128
V7X_SUBLANES = 8
V7X_VMEM_LIMIT = 56 * 1024 * 1024
W_LOG_SHIFT = 0.5
ATTN_KV_BLOCK_BYTES = 4 * 1024 * 1024
SCAN_J_BLOCK = 8


def _cp(sem, vmem=V7X_VMEM_LIMIT):
    return pltpu.CompilerParams(dimension_semantics=sem, vmem_limit_bytes=vmem)


def _tile(n, cap, mult=16):
    best = None
    for t in range(mult, min(n, cap) + 1, mult):
        if n % t == 0:
            best = t
    return best if best is not None else n


def _round_up(n, m):
    return (n + m - 1) // m * m


def _rms(x, g):
    ms = jnp.mean(x * x, axis=-1, keepdims=True)
    return (x * lax.rsqrt(ms + EPS)) * g


def _norm2_kernel(xp_ref, xs_ref, g_ref, x_ref, h_ref, *, n_prompt_tiles):
    i = pl.program_id(0)

    def emit(x):
        x_ref[...] = x
        h_ref[...] = _rms(x, g_ref[...]).astype(BF16)

    @pl.when(i < n_prompt_tiles)
    def _():
        emit(xp_ref[...])

    @pl.when(i >= n_prompt_tiles)
    def _():
        emit(xs_ref[...])


def _norm_concat(xp, xs, g):
    (n_p, d), n_s = xp.shape, xs.shape[0]
    tr = _tile(_gcd(n_p, n_s), 256, 8)
    npt, nst = n_p // tr, n_s // tr
    n = n_p + n_s
    return pl.pallas_call(
        functools.partial(_norm2_kernel, n_prompt_tiles=npt),
        grid=(npt + nst,),
        in_specs=[
            pl.BlockSpec((tr, d), lambda i: (jnp.minimum(i, npt - 1), 0)),
            pl.BlockSpec((tr, d), lambda i: (jnp.maximum(i - npt, 0), 0)),
            pl.BlockSpec((1, d), lambda i: (0, 0)),
        ],
        out_specs=[pl.BlockSpec((tr, d), lambda i: (i, 0)),
                   pl.BlockSpec((tr, d), lambda i: (i, 0))],
        out_shape=[jax.ShapeDtypeStruct((n, d), F32), jax.ShapeDtypeStruct((n, d), BF16)],
        compiler_params=_cp(("arbitrary",)),
        name="norm_concat",
    )(xp, xs, g.reshape(1, d))


def _gcd(a, b):
    while b:
        a, b = b, a % b
    return a


def _lcm(a, b):
    return a * b // _gcd(a, b)


def _norm_kernel(x_ref, g_ref, h_ref):
    h_ref[...] = _rms(x_ref[...], g_ref[...]).astype(h_ref.dtype)


def _norm(x, g, out_dtype=BF16, row0=0, rows=None):
    n, d = x.shape
    rows = n if rows is None else rows
    tr = _tile(_gcd(row0, rows) if row0 else rows, 256, 16)
    off = row0 // tr
    return pl.pallas_call(
        _norm_kernel,
        grid=(rows // tr,),
        in_specs=[pl.BlockSpec((tr, d), lambda i: (off + i, 0)),
                  pl.BlockSpec((1, d), lambda i: (0, 0))],
        out_specs=pl.BlockSpec((tr, d), lambda i: (i, 0)),
        out_shape=jax.ShapeDtypeStruct((rows, d), out_dtype),
        compiler_params=_cp(("arbitrary",)),
        name="norm",
    )(x, g.reshape(1, d))


def _gateup_kernel(h_ref, wg_ref, wu_ref, o_ref):
    h = h_ref[...]
    gate = jnp.dot(h, wg_ref[...].astype(BF16), preferred_element_type=F32)
    up = jnp.dot(h, wu_ref[...].astype(BF16), preferred_element_type=F32)
    o_ref[...] = ((gate * jax.nn.sigmoid(gate)) * up * 0.5).astype(o_ref.dtype)


def _gateup(h, w_in):
    n, d = h.shape
    f = w_in.shape[1] // 2
    tm = _tile(n, 1100)
    tn = _tile(f, 256, V7X_LANES)
    nf = f // tn
    return pl.pallas_call(
        _gateup_kernel,
        grid=(n // tm, nf),
        in_specs=[
            pl.BlockSpec((tm, d), lambda i, j: (i, 0)),
            pl.BlockSpec((d, tn), lambda i, j: (0, j)),
            pl.BlockSpec((d, tn), lambda i, j: (0, nf + j)),
        ],
        out_specs=pl.BlockSpec((tm, tn), lambda i, j: (i, j)),
        out_shape=jax.ShapeDtypeStruct((n, f), BF16),
        compiler_params=_cp(("arbitrary", "arbitrary")),
        name="ffn_gateup",
    )(h, w_in, w_in)


def _mm_kernel(x_ref, w_ref, o_ref, *, act):
    acc = jnp.dot(x_ref[...], w_ref[...].astype(BF16), preferred_element_type=F32)
    if act == "sigmoid":
        acc = jax.nn.sigmoid(acc)
    o_ref[...] = acc.astype(o_ref.dtype)


def _mm_res_kernel(x_ref, w_ref, r_ref, o_ref):
    o_ref[...] = r_ref[...] + jnp.dot(x_ref[...], w_ref[...].astype(BF16), preferred_element_type=F32)


_NT_DIMS = (((1,), (1,)), ((), ()))


def _mm_nt_kernel(x_ref, w_ref, o_ref, *, act):
    acc = lax.dot_general(x_ref[...], w_ref[...].astype(BF16), _NT_DIMS, preferred_element_type=F32)
    if act == "sigmoid":
        acc = jax.nn.sigmoid(acc)
    o_ref[...] = acc.astype(o_ref.dtype)


def _mm_nt_shift_kernel(x_ref, wa_ref, wb_ref, o_ref, w_scr, *, act):
    @pl.when(pl.program_id(1) == 0)
    def _():
        head = wa_ref.shape[0]
        w_scr[:head] = wa_ref[...].astype(BF16)
        w_scr[head:] = wb_ref[:w_scr.shape[0] - head].astype(BF16)

    acc = lax.dot_general(x_ref[...], w_scr[...], _NT_DIMS, preferred_element_type=F32)
    if act == "sigmoid":
        acc = jax.nn.sigmoid(acc)
    o_ref[...] = acc.astype(o_ref.dtype)


def _matmul_nt(x, wt, *, row0=0, rows=None, out_dtype=F32, act=None, tn=512):
    m, k = x.shape
    rows = wt.shape[0] - row0 if rows is None else rows
    tn = min(tn, _round_up(rows, V7X_LANES))
    nb = -(-rows // tn)
    tm = _tile(m, 1100)
    out_shape = jax.ShapeDtypeStruct((m, nb * tn), out_dtype)
    if row0 % tn == 0:
        jb = row0 // tn
        return pl.pallas_call(
            functools.partial(_mm_nt_kernel, act=act),
            grid=(m // tm, nb),
            in_specs=[pl.BlockSpec((tm, k), lambda i, j: (i, 0)),
                      pl.BlockSpec((tn, k), lambda i, j: (jb + j, 0))],
            out_specs=pl.BlockSpec((tm, tn), lambda i, j: (i, j)),
            out_shape=out_shape,
            compiler_params=_cp(("arbitrary", "arbitrary")),
            name="matmul_nt",
        )(x, wt)
    head = tn - row0 % tn
    assert head % 16 == 0 and row0 % head == 0 and tn % head == 0
    ja, jb, step = row0 // head, (row0 + head) // tn, tn // head
    return pl.pallas_call(
        functools.partial(_mm_nt_shift_kernel, act=act),
        grid=(nb, m // tm),
        in_specs=[pl.BlockSpec((tm, k), lambda j, i: (i, 0)),
                  pl.BlockSpec((head, k), lambda j, i: (ja + step * j, 0)),
                  pl.BlockSpec((tn, k), lambda j, i: (jb + j, 0))],
        out_specs=pl.BlockSpec((tm, tn), lambda j, i: (i, j)),
        out_shape=out_shape,
        scratch_shapes=[pltpu.VMEM((tn, k), BF16)],
        compiler_params=_cp(("arbitrary", "arbitrary")),
        name="matmul_nt_shifted",
    )(x, wt, wt)


def _matmul(x, w, *, col0=0, cols=None, out_dtype=F32, act=None, residual=None, tn=512,
            x_resident=False):
    m, k = x.shape
    cols = w.shape[1] - col0 if cols is None else cols
    tn = min(tn, _round_up(cols, V7X_LANES))
    assert col0 % tn == 0
    nb, jb = -(-cols // tn), col0 // tn
    tm = _tile(m, 1100)
    x_mode = pl.Buffered(1) if x_resident else None
    in_specs = [pl.BlockSpec((tm, k), lambda i, j: (i, 0), pipeline_mode=x_mode),
                pl.BlockSpec((k, tn), lambda i, j: (0, jb + j))]
    args = [x, w]
    if residual is None:
        kern = functools.partial(_mm_kernel, act=act)
    else:
        kern = _mm_res_kernel
        in_specs.append(pl.BlockSpec((tm, tn), lambda i, j: (i, j)))
        args.append(residual)
    return pl.pallas_call(
        kern,
        grid=(m // tm, nb),
        in_specs=in_specs,
        out_specs=pl.BlockSpec((tm, tn), lambda i, j: (i, j)),
        out_shape=jax.ShapeDtypeStruct((m, nb * tn), out_dtype),
        compiler_params=_cp(("arbitrary", "arbitrary")),
        name="matmul",
    )(*args)


def _shift_rows(x, k, tile_row0, seq_len, halo):
    rows = lax.broadcasted_iota(jnp.int32, x.shape, 0)
    out = pltpu.roll(x, k, axis=0)
    at_start = (tile_row0 % seq_len) == 0
    for q in range(k):
        h = halo[V7X_SUBLANES - k + q:V7X_SUBLANES - k + q + 1, :]
        h = jnp.where(at_start, jnp.zeros_like(h), h)
        out = jnp.where(rows == q, h, out)
    return out


def _lowrank_maps(low, param_refs, low_valid):
    _, w0_ref, a0_ref, lw_ref, la_ref, lg_ref = param_refs
    low = jnp.where(lax.broadcasted_iota(jnp.int32, low.shape, 1) < low_valid, low, 0.0)
    z = w0_ref[...] + jnp.dot(jnp.tanh(low).astype(BF16), lw_ref[...], preferred_element_type=F32)
    decay = jnp.exp(-jnp.exp(-W_LOG_SHIFT) * jax.nn.sigmoid(z))
    a = jax.nn.sigmoid(a0_ref[...] + jnp.dot(low.astype(BF16), la_ref[...], preferred_element_type=F32))
    g = jnp.dot(jax.nn.sigmoid(low).astype(BF16), lg_ref[...], preferred_element_type=F32)
    return decay, a, g


def _premix(x, prev, param_refs, d_a, low_valid):
    ps = x + (prev - x) * param_refs[0][...]
    decay, a, g = _lowrank_maps(ps[:, 3 * d_a:], param_refs, low_valid)
    return ps[:, :d_a], decay, ps[:, d_a:2 * d_a], ps[:, 2 * d_a:3 * d_a], a, g


def _rwkv_pre_first_kernel(x_ref, first_ref, *refs, d_a, low_valid, seq_len):
    x = x_ref[...]
    t = lax.broadcasted_iota(jnp.int32, x.shape, 0) % seq_len
    prev = jnp.where(t == 0, first_ref[...], pltpu.roll(x, 1, axis=0))
    for o_ref, val in zip(refs[6:], _premix(x, prev, refs[:6], d_a, low_valid)):
        o_ref[...] = val


def _rwkv_pre_first(p_a, row0, rows, seq_len, first, params, low_valid):
    pw = p_a.shape[1]
    d_a = params[1].shape[1]
    tr = _tile(rows, 128, _lcm(seq_len, V7X_SUBLANES))
    assert row0 % tr == 0 and rows % tr == 0
    off = row0 // tr
    full = lambda a: pl.BlockSpec(a.shape, lambda i: (0, 0))
    out_spec = pl.BlockSpec((tr, d_a), lambda i: (i, 0))
    return pl.pallas_call(
        functools.partial(_rwkv_pre_first_kernel, d_a=d_a, low_valid=low_valid, seq_len=seq_len),
        grid=(rows // tr,),
        in_specs=[pl.BlockSpec((tr, pw), lambda i: (off + i, 0)),
                  pl.BlockSpec((tr, pw), lambda i: (i, 0))] + [full(p) for p in params],
        out_specs=[out_spec] * 6,
        out_shape=[jax.ShapeDtypeStruct((rows, d_a), F32)] * 6,
        compiler_params=_cp(("arbitrary",)),
        name="rwkv_pre_first",
    )(p_a, first, *params)


def _chans_to_lanes(xs, n_heads):
    bg = len(xs)
    ln = bg * n_heads
    group = lax.broadcasted_iota(jnp.int32, (xs[0].shape[0], ln), 1) // n_heads
    out = []
    for q in range(HEAD_A // bg):
        chunk = [x[:, q * ln:(q + 1) * ln] for x in xs]
        rolled = [[c if s == 0 else pltpu.roll(c, s * n_heads, axis=1) for s in range(bg)] for c in chunk]
        for jl in range(bg):
            acc = rolled[bg - 1][(bg - 1 - jl) % bg]
            for b in range(bg - 2, -1, -1):
                acc = jnp.where(group == b, rolled[b][(b - jl) % bg], acc)
            out.append(acc)
    return out


def _rwkv_pre_lanes_kernel(*refs, bg, d_a, low_valid, tr, seq_len):
    x_refs, halo_refs = refs[:bg], refs[bg:2 * bg]
    param_refs = refs[2 * bg:2 * bg + 6]
    r_ref, w_ref, k_ref, a_ref, v_ref, g_ref, tail_ref = refs[2 * bg + 6:]
    n_heads = d_a // HEAD_A
    shifted = []
    for b in range(bg):
        x = x_refs[b][...]
        prev = _shift_rows(x, 1, pl.program_id(1) * tr, seq_len, halo_refs[b][...])
        shifted.append(x + (prev - x) * param_refs[0][...])
        tail_ref[b] = x[tr - V7X_SUBLANES:, :]
    low = jnp.concatenate([ps[:, 3 * d_a:] for ps in shifted], axis=0)
    decay, a, g = _lowrank_maps(low, param_refs, low_valid)
    mixed = []
    for b, ps in enumerate(shifted):
        rows = slice(b * tr, (b + 1) * tr)
        v_ref[b] = ps[:, 2 * d_a:3 * d_a]
        g_ref[b] = g[rows]
        mixed.append((ps[:, :d_a], decay[rows], ps[:, d_a:2 * d_a], a[rows]))
    for o_ref, vals in zip((r_ref, w_ref, k_ref, a_ref), zip(*mixed)):
        for j, val in enumerate(_chans_to_lanes(vals, n_heads)):
            o_ref[0, j] = val


def _rwkv_pre_lanes(p_a, row0, batch, seq_len, params, low_valid, bg):
    pw = p_a.shape[1]
    d_a = params[1].shape[1]
    ln = bg * (d_a // HEAD_A)
    tr = _tile(seq_len, 32, V7X_SUBLANES)
    assert row0 % tr == 0 and batch % bg == 0
    off, tpb, hb = row0 // tr, seq_len // tr, tr // V7X_SUBLANES
    xs = [pl.BlockSpec((tr, pw), lambda g, i, b=b: (off + (g * bg + b) * tpb + i, 0)) for b in range(bg)]
    halos = [pl.BlockSpec((V7X_SUBLANES, pw),
                          lambda g, i, b=b: (jnp.maximum((off + (g * bg + b) * tpb + i) * hb - 1, 0), 0))
             for b in range(bg)]
    full = lambda a: pl.BlockSpec(a.shape, lambda g, i: (0, 0))
    lanes = pl.BlockSpec((1, HEAD_A, tr, ln), lambda g, i: (g, 0, i, 0))
    rowsp = pl.BlockSpec((bg, tr, d_a), lambda g, i: (g, i, 0))
    lanes_shape = jax.ShapeDtypeStruct((batch // bg, HEAD_A, seq_len, ln), F32)
    rows_shape = jax.ShapeDtypeStruct((batch, seq_len, d_a), F32)
    tail = pl.BlockSpec((bg, V7X_SUBLANES, pw), lambda g, i: (g, 0, 0))
    r, w, k, a, v, g, tails = pl.pallas_call(
        functools.partial(_rwkv_pre_lanes_kernel, bg=bg, d_a=d_a, low_valid=low_valid, tr=tr,
                          seq_len=seq_len),
        grid=(batch // bg, tpb),
        in_specs=xs + halos + [full(p) for p in params],
        out_specs=[lanes] * 4 + [rowsp] * 2 + [tail],
        out_shape=[lanes_shape] * 4 + [rows_shape] * 2
        + [jax.ShapeDtypeStruct((batch, V7X_SUBLANES, pw), F32)],
        compiler_params=_cp(("arbitrary", "arbitrary")),
        name="rwkv_pre_lanes",
    )(*([p_a] * (2 * bg)), *params)
    return (r, w, k, a, v.reshape(batch * seq_len, d_a), g.reshape(batch * seq_len, d_a),
            tails[:, V7X_SUBLANES - 1])


def _scan_kernel(r_ref, w_ref, k_ref, a_ref, v_ref, kk_ref, ka_ref, rk_ref, lnw_ref, lnb_ref,
                 s0_ref, y_ref, s_ref, av_ref, bv_ref, km_ref, bonus_ref, *, tc):
    hd = s_ref.shape[1]
    row = lambda ref, j: ref[pl.ds(j, 1), :]

    @pl.when(pl.program_id(1) == 0)
    def _():
        s_ref[...] = s0_ref[...]

    ss = jnp.zeros(bonus_ref.shape, F32)
    for j in range(hd):
        kk = k_ref[0, j] * row(kk_ref, j)
        ss = ss + kk * kk
    inv = 1.0 / jnp.maximum(jnp.sqrt(ss), 1e-12)
    bonus = jnp.zeros(bonus_ref.shape, F32)
    for j in range(hd):
        k, a = k_ref[0, j], a_ref[0, j]
        kk = (k * row(kk_ref, j)) * inv
        kmod = k * (1.0 + (a - 1.0) * row(ka_ref, j))
        av_ref[j] = -kk
        bv_ref[j] = kk * a
        km_ref[j] = kmod
        bonus = bonus + (r_ref[0, j] * kmod) * row(rk_ref, j)
    bonus_ref[...] = bonus

    def step(t, carry):
        at = pl.ds(t, 1)
        v = v_ref[0, t]

        def project(jb, sa):
            for jj in range(SCAN_J_BLOCK):
                j = jb * SCAN_J_BLOCK + jj
                sa = sa + s_ref[0, j] * av_ref[j, at, :]
            return sa

        sa = lax.fori_loop(0, hd // SCAN_J_BLOCK, project, jnp.zeros_like(v))

        def update(jb, y):
            for jj in range(SCAN_J_BLOCK):
                j = jb * SCAN_J_BLOCK + jj
                sj = (s_ref[0, j] * w_ref[0, j, at, :] + sa * bv_ref[j, at, :]) + v * km_ref[j, at, :]
                s_ref[0, j] = sj
                y = y + sj * r_ref[0, j, at, :]
            return y

        y = lax.fori_loop(0, hd // SCAN_J_BLOCK, update, jnp.zeros_like(v))
        mu = jnp.mean(y, axis=0, keepdims=True)
        yc = y - mu
        var = jnp.mean(yc * yc, axis=0, keepdims=True)
        yn = (yc * lax.rsqrt(var + GN_EPS)) * lnw_ref[...] + lnb_ref[...]
        y_ref[0, t] = yn + bonus_ref[at, :] * v
        return carry

    lax.fori_loop(0, tc, step, 0)


def _scan(r, w, k, a, v, params, s0):
    g, hd, t, ln = r.shape
    tc = _tile(t, 64, 1)
    by_chan = pl.BlockSpec((1, hd, tc, ln), lambda gi, c: (gi, 0, c, 0))
    by_time = pl.BlockSpec((1, tc, hd, ln), lambda gi, c: (gi, c, 0, 0))
    per_group = params[0].shape[0] > 1
    par = pl.BlockSpec((None, hd, ln), lambda gi, c: (gi if per_group else 0, 0, 0))
    st = pl.BlockSpec((1, hd, hd, ln), lambda gi, c: (gi, 0, 0, 0))
    return pl.pallas_call(
        functools.partial(_scan_kernel, tc=tc),
        grid=(g, t // tc),
        in_specs=[by_chan] * 4 + [by_time] + [par] * 5 + [st],
        out_specs=[by_time, st],
        out_shape=[jax.ShapeDtypeStruct(v.shape, F32), jax.ShapeDtypeStruct(s0.shape, F32)],
        scratch_shapes=[pltpu.VMEM((hd, tc, ln), F32)] * 3 + [pltpu.VMEM((tc, ln), F32)],
        compiler_params=_cp(("arbitrary", "arbitrary")),
        name="rwkv_scan",
    )(r, w, k, a, v, *params, s0)


class _Lanes:
    def __init__(self, b, n_heads):
        self.h = n_heads
        self.by_batch = b % V7X_LANES == 0
        self.bi = V7X_LANES if self.by_batch else max(1, min(b, V7X_LANES // n_heads))
        self.bo = b // self.bi
        self.groups = self.bo * n_heads if self.by_batch else self.bo
        self.lanes = self.bi if self.by_batch else self.bi * n_heads

    def rows_by_time(self, x2d, t):
        x = x2d.reshape(self.bo, self.bi, t, self.h, HEAD_A)
        x = x.transpose(0, 3, 2, 4, 1) if self.by_batch else x.transpose(0, 2, 4, 1, 3)
        return x.reshape(self.groups, t, HEAD_A, self.lanes)

    def rows_by_chan(self, x2d, t):
        x = x2d.reshape(self.bo, self.bi, t, HEAD_A, self.h)
        x = x.transpose(0, 4, 3, 2, 1) if self.by_batch else x.transpose(0, 3, 2, 1, 4)
        return x.reshape(self.groups, HEAD_A, t, self.lanes)

    def rows_back(self, y, t):
        if self.by_batch:
            y = y.reshape(self.bo, self.h, t, HEAD_A, self.bi).transpose(0, 4, 2, 1, 3)
        else:
            y = y.reshape(self.bo, t, HEAD_A, self.bi, self.h).transpose(0, 3, 1, 4, 2)
        return y.reshape(self.bo * self.bi * t, self.h * HEAD_A)

    def param(self, p):
        p = p.reshape(self.h, HEAD_A)
        if self.by_batch:
            p = jnp.broadcast_to(p[None, :, :, None], (self.bo, self.h, HEAD_A, self.bi))
            return p.reshape(self.groups, HEAD_A, self.lanes)
        return jnp.tile(p.T, (1, self.bi))[None]

    def state_in(self, s):
        s = s.astype(F32).reshape(self.bo, self.bi, self.h, HEAD_A, HEAD_A)
        s = s.transpose(0, 2, 4, 3, 1) if self.by_batch else s.transpose(0, 4, 3, 1, 2)
        return s.reshape(self.groups, HEAD_A, HEAD_A, self.lanes)

    def state_out(self, s):
        if self.by_batch:
            s = s.reshape(self.bo, self.h, HEAD_A, HEAD_A, self.bi).transpose(0, 4, 1, 3, 2)
        else:
            s = s.reshape(self.bo, HEAD_A, HEAD_A, self.bi, self.h).transpose(0, 3, 4, 2, 1)
        return s.reshape(self.bo * self.bi, self.h, HEAD_A, HEAD_A)


def _rwkv_group(lay, rwka, v, t, s_prev, scan_params):
    params = [lay.param(p) for p in scan_params]
    if s_prev is None:
        s0 = jnp.zeros((lay.groups, HEAD_A, HEAD_A, lay.lanes), F32)
    else:
        s0 = lay.state_in(s_prev)
    y, s_new = _scan(*rwka, lay.rows_by_time(v, t), params, s0)
    return lay.rows_back(y, t), lay.state_out(s_new)


def _gate_kernel(*refs):
    y_ref, g_ref, o_ref = refs[-3:]
    o_ref[...] = (y_ref[...] * g_ref[...]).astype(o_ref.dtype)


def _gate_rows(y, g, row0, prev_out):
    rows, d_a = y.shape
    tr = _tile(_gcd(row0, rows) if row0 else rows, 512, 16)
    off = row0 // tr
    row = pl.BlockSpec((tr, d_a), lambda i: (i, 0))
    return pl.pallas_call(
        _gate_kernel,
        grid=(rows // tr,),
        in_specs=[pl.BlockSpec(memory_space=pl.ANY), row, row],
        out_specs=pl.BlockSpec((tr, d_a), lambda i: (off + i, 0)),
        out_shape=jax.ShapeDtypeStruct(prev_out.shape, BF16),
        input_output_aliases={0: 0},
        compiler_params=_cp(("arbitrary",)),
        name="rwkv_gate",
    )(prev_out, y, g)


def _conv_kernel(*refs, tr, seq_len, mode):
    gb_ref, gc_ref, u_ref, aux1_ref, aux2_ref, cw_ref = refs[-8:-2]
    y_ref, u2_ref = refs[-2:]
    f1_ref, f2_ref = aux1_ref, aux2_ref
    u2 = gc_ref[...] * u_ref[...]
    if mode == "halo":
        halo = aux1_ref[...] * aux2_ref[...]
        row0 = pl.program_id(0) * tr
        p1 = _shift_rows(u2, 1, row0, seq_len, halo)
        p2 = _shift_rows(u2, 2, row0, seq_len, halo)
    else:
        t = lax.broadcasted_iota(jnp.int32, u2.shape, 0) % seq_len
        p1 = jnp.where(t < 1, f1_ref[...], pltpu.roll(u2, 1, axis=0))
        p2 = jnp.where(t < 2, f2_ref[...], pltpu.roll(u2, 2, axis=0))
    z = p2 * cw_ref[0:1, :] + p1 * cw_ref[1:2, :]
    z = z + u2 * cw_ref[2:3, :]
    y_ref[...] = (gb_ref[...] * z).astype(y_ref.dtype)
    u2_ref[...] = u2


def _conv(p_bc, row0, rows, seq_len, firsts, conv_w, prev_out):
    n = p_bc.shape[0]
    d_b = conv_w.shape[1]
    mode = "halo" if firsts is None else "first"
    if mode == "halo":
        tr = _tile(_gcd(seq_len, rows), 256, 16)
    else:
        tr = _tile(rows, 256, _lcm(seq_len, 16))
    assert row0 % tr == 0 and rows % tr == 0 and seq_len >= CONV_W - 1
    off = row0 // tr
    col = lambda c: pl.BlockSpec((tr, d_b), lambda i: (off + i, c))
    in_specs = [col(0), col(1), col(2)]
    args = [p_bc, p_bc, p_bc]
    if mode == "halo":
        hb = tr // V7X_SUBLANES
        hcol = lambda c: pl.BlockSpec(
            (V7X_SUBLANES, d_b), lambda i: (jnp.maximum((off + i) * hb - 1, 0), c))
        in_specs += [hcol(1), hcol(2)]
        args += [p_bc, p_bc]
    else:
        in_specs += [pl.BlockSpec((tr, d_b), lambda i: (i, 0))] * 2
        args += list(firsts)
    in_specs.append(pl.BlockSpec(conv_w.shape, lambda i: (0, 0)))
    args.append(conv_w)
    in_specs = [pl.BlockSpec(memory_space=pl.ANY)] * 2 + in_specs
    args = list(prev_out) + args
    out_spec = pl.BlockSpec((tr, d_b), lambda i: (off + i, 0))
    return pl.pallas_call(
        functools.partial(_conv_kernel, tr=tr, seq_len=seq_len, mode=mode),
        grid=(rows // tr,),
        in_specs=in_specs,
        out_specs=[out_spec, out_spec],
        out_shape=[jax.ShapeDtypeStruct((n, d_b), BF16), jax.ShapeDtypeStruct((n, d_b), F32)],
        input_output_aliases={0: 0, 1: 1},
        compiler_params=_cp(("arbitrary",)),
        name="conv_" + mode,
    )(*args)


def _attn_kernel(*refs, bb, tq, heads):
    q_ref, k_ref, v_ref, o_ref = refs[-4:]
    dh = q_ref.shape[1] // heads
    scale = dh ** -0.5
    row_batch = lax.broadcasted_iota(jnp.int32, (bb * tq, dh), 0) // tq
    for h in range(heads):
        cols = slice(h * dh, (h + 1) * dh)
        q = q_ref[:, cols].astype(BF16)
        out = None
        for b in range(bb):
            k = k_ref[b, :, cols].astype(BF16)
            v = v_ref[b, :, cols].astype(BF16)
            s = lax.dot_general(q, k, (((1,), (1,)), ((), ())), preferred_element_type=F32) * scale
            p = jnp.exp(s - jnp.max(s, axis=-1, keepdims=True))
            p = p / jnp.sum(p, axis=-1, keepdims=True)
            o = jnp.dot(p.astype(BF16), v, preferred_element_type=F32)
            out = o if out is None else jnp.where(row_batch == b, o, out)
        o_ref[:, cols] = out.astype(o_ref.dtype)


def _attention(q_src, q_col, row0, b, t, mem_k, mem_v, heads, prev_out):
    n_mem, d_c = mem_k.shape[1:]
    if t >= 128:
        bb, tq = 1, _tile(t, 512, 16)
    else:
        kv_block_cap = max(1, ATTN_KV_BLOCK_BYTES // (n_mem * d_c * 4))
        bb, tq = _tile(b, min(kv_block_cap, max(1, 64 // t)), 1), t
    assert (bb * tq) % 16 == 0 and row0 % (bb * tq) == 0
    tpb = t // tq
    off = row0 // (bb * tq)
    qmap = lambda i, j: (off + i * tpb + j, q_col)
    omap = lambda i, j: (off + i * tpb + j, 0)
    kv = pl.BlockSpec((bb, n_mem, d_c), lambda i, j: (i, 0, 0))
    return pl.pallas_call(
        functools.partial(_attn_kernel, bb=bb, tq=tq, heads=heads),
        grid=(b // bb, tpb),
        in_specs=[pl.BlockSpec(memory_space=pl.ANY), pl.BlockSpec((bb * tq, d_c), qmap), kv, kv],
        out_specs=pl.BlockSpec((bb * tq, d_c), omap),
        out_shape=jax.ShapeDtypeStruct(prev_out.shape, BF16),
        input_output_aliases={0: 0},
        compiler_params=_cp(("arbitrary", "arbitrary")),
        name="mem_attention",
    )(prev_out, q_src, mem_k, mem_v)


def _merge_kernel(ya_ref, yb_ref, yc_ref, ga_ref, gb_ref, gc_ref, wa_ref, wb_ref, wc_ref, o_ref):
    def branch(y_ref, g_ref, w_ref):
        return g_ref[...].astype(F32) * jnp.dot(
            y_ref[...], w_ref[...].astype(BF16), preferred_element_type=F32)

    acc = branch(ya_ref, ga_ref, wa_ref) + branch(yb_ref, gb_ref, wb_ref)
    o_ref[...] = (acc + branch(yc_ref, gc_ref, wc_ref)).astype(o_ref.dtype)


def _merge(ya, yb, yc, gates, wa, wb, wc):
    n, d = ya.shape[0], wa.shape[1]
    tm = _tile(n, 1100)
    tn = _tile(d, 512, V7X_LANES)
    nj = d // tn
    act = lambda y: pl.BlockSpec((tm, y.shape[1]), lambda i, j: (i, 0))
    gate = lambda c: pl.BlockSpec((tm, tn), lambda i, j: (i, c * nj + j))
    wsp = lambda w: pl.BlockSpec((w.shape[0], tn), lambda i, j: (0, j))
    return pl.pallas_call(
        _merge_kernel,
        grid=(n // tm, nj),
        in_specs=[act(ya), act(yb), act(yc), gate(0), gate(1), gate(2), wsp(wa), wsp(wb), wsp(wc)],
        out_specs=pl.BlockSpec((tm, tn), lambda i, j: (i, j)),
        out_shape=jax.ShapeDtypeStruct((n, d), BF16),
        compiler_params=_cp(("arbitrary", "arbitrary")),
        name="branch_merge",
    )(ya, yb, yc, gates, gates, gates, wa, wb, wc)


def _swap_head_chan(p, n_heads, to_chan_major, axis=-1):
    axis = axis % p.ndim
    blocks = p.shape[axis] // (n_heads * HEAD_A)
    inner = (n_heads, HEAD_A) if to_chan_major else (HEAD_A, n_heads)
    q = p.reshape(p.shape[:axis] + (blocks,) + inner + p.shape[axis + 1:])
    return jnp.swapaxes(q, axis + 1, axis + 2).reshape(p.shape)


def _pad_rows(w, row0, total):
    return jnp.zeros((total, w.shape[1]), BF16).at[row0:row0 + w.shape[0]].set(w.astype(BF16))


def _first_rows(state, seq_len, width):
    b, k, c = state.shape
    out = jnp.zeros((b, seq_len, width), F32).at[:, :k, :c].set(state.astype(F32))
    return out.reshape(b * seq_len, width)


def _layer(x, h1, groups, lp, mem_kv, heads_c):
    n, d = x.shape
    d_a = lp["w0"].shape[0]
    d_b = lp["conv_w"].shape[1]
    p_a_w = lp["mu_shift"].shape[0]
    lora_w, lora_a, lora_g = (lp[q].shape[0] for q in ("w_lora_up", "a_lora_up", "g_lora_up"))
    d_c = lp["w_br_c"].shape[0]
    assert d_c == d_b and p_a_w == 3 * d_a + lora_w + lora_a + lora_g

    x = _matmul(_gateup(h1, lp["w_ffn1_in"]), lp["w_ffn1_out"], residual=x, tn=256, x_resident=True)
    h = _norm(x, lp["g_mix"])

    w_t = jnp.swapaxes(lp["w_in"], 0, 1)
    pw = _round_up(p_a_w, 512)
    w_a = jnp.concatenate([_swap_head_chan(w_t[:2 * d_a], d_a // HEAD_A, to_chan_major=True, axis=0),
                           w_t[2 * d_a:pw]], axis=0)
    p_a = _matmul_nt(h, w_a)
    p_bc = _matmul_nt(h, w_t, row0=p_a_w, rows=3 * d_b + d_c)
    gates = _matmul_nt(h, w_t, row0=p_a_w + 3 * d_b + d_c, rows=3 * d, out_dtype=BF16, act="sigmoid")

    pw = p_a.shape[1]
    low_w = pw - 3 * d_a
    n_heads = d_a // HEAD_A
    rk_swap = lambda p: jnp.concatenate(
        [_swap_head_chan(p[..., :2 * d_a], n_heads, to_chan_major=True), p[..., 2 * d_a:]], axis=-1)
    to_cm = lambda p: _swap_head_chan(p, n_heads, to_chan_major=True)
    pre_params = (
        jnp.pad(rk_swap(lp["mu_shift"]), (0, pw - p_a_w)).reshape(1, pw),
        to_cm(lp["w0"]).reshape(1, d_a), to_cm(lp["a0"]).reshape(1, d_a),
        _pad_rows(to_cm(lp["w_lora_up"]), 0, low_w),
        _pad_rows(to_cm(lp["a_lora_up"]), lora_w, low_w),
        _pad_rows(lp["g_lora_up"], lora_w + lora_a, low_w),
    )
    scan_params = (lp["k_k"], lp["k_a"], lp["r_k"].reshape(-1), lp["lnx_w"], lp["lnx_b"])
    ya, new_states = jnp.zeros((n, d_a), BF16), []
    for row0, b, t, st in groups:
        lay = _Lanes(b, n_heads)
        if st is None:
            assert not lay.by_batch
            *rwka, v, g, last = _rwkv_pre_lanes(p_a, row0, b, t, pre_params, p_a_w - 3 * d_a, lay.bi)
            last = last[:, :p_a_w]
        else:
            first = _first_rows(rk_swap(st["shift"])[:, None, :], t, pw)
            r, w, k, v, a, g = _rwkv_pre_first(p_a, row0, b * t, t, first, pre_params, p_a_w - 3 * d_a)
            rwka = [lay.rows_by_chan(q, t) for q in (r, w, k, a)]
            last = p_a[row0 + t - 1:row0 + b * t:t, :p_a_w]
        y, s_new = _rwkv_group(lay, rwka, v, t, None if st is None else st["rwkv"], scan_params)
        ya = _gate_rows(y, g, row0, ya)
        shift_new = jnp.concatenate(
            [_swap_head_chan(last[:, :2 * d_a], n_heads, to_chan_major=False), last[:, 2 * d_a:]], axis=1)
        new_states.append({"rwkv": s_new, "shift": shift_new})

    conv = (jnp.zeros((n, d_b), BF16), jnp.zeros((n, d_b), F32))
    for row0, b, t, st in groups:
        firsts = None
        if st is not None:
            buf = st["conv"]
            firsts = (_first_rows(buf[:, 1:, :], t, d_b), _first_rows(buf, t, d_b))
        conv = _conv(p_bc, row0, b * t, t, firsts, lp["conv_w"], conv)
    yb, u2 = conv
    for (row0, b, t, st), ns in zip(groups, new_states):
        tail = [u2[row0 + t - q:row0 + b * t:t] for q in range(CONV_W - 1, 0, -1)]
        ns["conv"] = jnp.stack(tail, axis=1)

    yc = jnp.zeros((n, d_c), BF16)
    for (row0, b, t, st), (mk, mv) in zip(groups, mem_kv):
        yc = _attention(p_bc, (3 * d_b) // d_c, row0, b, t,
                        mk.reshape(b, -1, d_c), mv.reshape(b, -1, d_c), heads_c, yc)

    merged = _merge(ya, yb, yc, gates, lp["w_br_a"], lp["w_br_b"], lp["w_br_c"])
    x = _matmul(merged, lp["w_out"], residual=x)

    h = _norm(x, lp["g_ffn2"])
    x = _matmul(_gateup(h, lp["w_ffn2_in"]), lp["w_ffn2_out"], residual=x, tn=256, x_resident=True)
    return x, new_states


def kernel(x_prompt, x_sample, mem_prompt, cache_mem_k, cache_mem_v, state_rwkv, state_shift, state_conv, g_ffn1, w_ffn1_in, w_ffn1_out, g_mix, w_in, mu_shift, w0, w_lora_up, a0, a_lora_up, g_lora_up, k_k, k_a, r_k, lnx_w, lnx_b, conv_w, g_mem, w_mem_kv, w_br_a, w_br_b, w_br_c, w_out, g_ffn2, w_ffn2_in, w_ffn2_out, g_final):
    bp, tp, d = x_prompt.shape
    bs, ts, _ = x_sample.shape
    n_p, n_s = bp * tp, bs * ts
    depth = w_in.shape[0]
    n_mem, heads_c, dh_c = cache_mem_k.shape[2:]
    d_c = heads_c * dh_c

    per_layer = dict(
        g_ffn1=g_ffn1, w_ffn1_in=w_ffn1_in, w_ffn1_out=w_ffn1_out, g_mix=g_mix, w_in=w_in,
        mu_shift=mu_shift, w0=w0, w_lora_up=w_lora_up, a0=a0, a_lora_up=a_lora_up,
        g_lora_up=g_lora_up, k_k=k_k, k_a=k_a, r_k=r_k, lnx_w=lnx_w, lnx_b=lnx_b, conv_w=conv_w,
        w_br_a=w_br_a, w_br_b=w_br_b, w_br_c=w_br_c, w_out=w_out, g_ffn2=g_ffn2,
        w_ffn2_in=w_ffn2_in, w_ffn2_out=w_ffn2_out)

    mem_rows = mem_prompt.reshape(bp * n_mem, d)
    outs = {q: [] for q in ("mk", "mv", "rp", "shp", "cp", "rs", "shs", "cs")}
    x = None
    for l in range(depth):
        lp = {q: p[l] for q, p in per_layer.items()}
        if l == 0:
            x, h1 = _norm_concat(x_prompt.reshape(n_p, d), x_sample.reshape(n_s, d), lp["g_ffn1"])
        else:
            h1 = _norm(x, lp["g_ffn1"])
        kv = _matmul(_norm(mem_rows, g_mem[l]), w_mem_kv[l])
        mk_p = kv[:, :d_c].reshape(bp, n_mem, heads_c, dh_c)
        mv_p = kv[:, d_c:2 * d_c].reshape(bp, n_mem, heads_c, dh_c)
        groups = [
            (0, bp, tp, None),
            (n_p, bs, ts, {"rwkv": state_rwkv[l], "shift": state_shift[l], "conv": state_conv[l]}),
        ]
        prompt_kv = (kv[:, :d_c].reshape(bp, n_mem, d_c), kv[:, d_c:2 * d_c].reshape(bp, n_mem, d_c))
        x, (st_p, st_s) = _layer(
            x, h1, groups, lp, [prompt_kv, (cache_mem_k[l], cache_mem_v[l])], heads_c)
        outs["mk"].append(mk_p)
        outs["mv"].append(mv_p)
        for tag, st in (("p", st_p), ("s", st_s)):
            outs["r" + tag].append(st["rwkv"])
            outs["sh" + tag].append(st["shift"])
            outs["c" + tag].append(st["conv"])
    y_prompt = _norm(x, g_final, F32, 0, n_p).reshape(bp, tp, d)
    y_sample = _norm(x, g_final, F32, n_p, n_s).reshape(bs, ts, d)
    stack = lambda q: jnp.stack(outs[q], 0)
    return (y_prompt, y_sample, stack("mk"), stack("mv"), stack("rp"), stack("shp"), stack("cp"),
            stack("rs"), stack("shs"), stack("cs"))
```

```python
import functools

import jax
import jax.numpy as jnp
from jax import lax
from jax.experimental import pallas as pl
from jax.experimental.pallas import tpu as pltpu

F32 = jnp.float32
BF16 = jnp.bfloat16

EPS = 1e-6
GN_EPS = 64e-5
HEAD_A = 64
CONV_W = 3
V7X_LANES = 128
V7X_SUBLANES = 8
V7X_VMEM_LIMIT = 56 * 1024 * 1024
W_LOG_SHIFT = 0.5
ATTN_KV_BLOCK_BYTES = 4 * 1024 * 1024
SCAN_J_BLOCK = 8
SCAN_FUSED_STEPS = 8
PROJECT_SLABS = 4


def _cp(sem, vmem=V7X_VMEM_LIMIT):
    return pltpu.CompilerParams(dimension_semantics=sem, vmem_limit_bytes=vmem)


def _tile(n, cap, mult=16):
    best = None
    for t in range(mult, min(n, cap) + 1, mult):
        if n % t == 0:
            best = t
    return best if best is not None else n


def _round_up(n, m):
    return (n + m - 1) // m * m


def _rms(x, g):
    ms = jnp.mean(x * x, axis=-1, keepdims=True)
    return (x * lax.rsqrt(ms + EPS)) * g


def _norm2_kernel(xp_ref, xs_ref, g_ref, x_ref, h_ref, *, n_prompt_tiles):
    i = pl.program_id(0)

    def emit(x):
        x_ref[...] = x
        h_ref[...] = _rms(x, g_ref[...]).astype(BF16)

    @pl.when(i < n_prompt_tiles)
    def _():
        emit(xp_ref[...])

    @pl.when(i >= n_prompt_tiles)
    def _():
        emit(xs_ref[...])


def _norm_concat(xp, xs, g):
    (n_p, d), n_s = xp.shape, xs.shape[0]
    tr = _tile(_gcd(n_p, n_s), 256, 8)
    npt, nst = n_p // tr, n_s // tr
    n = n_p + n_s
    return pl.pallas_call(
        functools.partial(_norm2_kernel, n_prompt_tiles=npt),
        grid=(npt + nst,),
        in_specs=[
            pl.BlockSpec((tr, d), lambda i: (jnp.minimum(i, npt - 1), 0)),
            pl.BlockSpec((tr, d), lambda i: (jnp.maximum(i - npt, 0), 0)),
            pl.BlockSpec((1, d), lambda i: (0, 0)),
        ],
        out_specs=[pl.BlockSpec((tr, d), lambda i: (i, 0)),
                   pl.BlockSpec((tr, d), lambda i: (i, 0))],
        out_shape=[jax.ShapeDtypeStruct((n, d), F32), jax.ShapeDtypeStruct((n, d), BF16)],
        compiler_params=_cp(("arbitrary",)),
        name="norm_concat",
    )(xp, xs, g.reshape(1, d))


def _gcd(a, b):
    while b:
        a, b = b, a % b
    return a


def _lcm(a, b):
    return a * b // _gcd(a, b)


def _norm_kernel(x_ref, g_ref, h_ref):
    h_ref[...] = _rms(x_ref[...], g_ref[...]).astype(h_ref.dtype)


def _norm(x, g, out_dtype=BF16, row0=0, rows=None):
    n, d = x.shape
    rows = n if rows is None else rows
    tr = _tile(_gcd(row0, rows) if row0 else rows, 256, 16)
    off = row0 // tr
    return pl.pallas_call(
        _norm_kernel,
        grid=(rows // tr,),
        in_specs=[pl.BlockSpec((tr, d), lambda i: (off + i, 0)),
                  pl.BlockSpec((1, d), lambda i: (0, 0))],
        out_specs=pl.BlockSpec((tr, d), lambda i: (i, 0)),
        out_shape=jax.ShapeDtypeStruct((rows, d), out_dtype),
        compiler_params=_cp(("arbitrary",)),
        name="norm",
    )(x, g.reshape(1, d))


def _gateup_kernel(h_ref, wg_ref, wu_ref, o_ref):
    h = h_ref[...]
    gate = jnp.dot(h, wg_ref[...].astype(BF16), preferred_element_type=F32)
    up = jnp.dot(h, wu_ref[...].astype(BF16), preferred_element_type=F32)
    o_ref[...] = ((gate * jax.nn.sigmoid(gate)) * up * 0.5).astype(o_ref.dtype)


def _gateup(h, w_in):
    n, d = h.shape
    f = w_in.shape[1] // 2
    tm = _tile(n, 1100)
    tn = _tile(f, 256, V7X_LANES)
    nf = f // tn
    return pl.pallas_call(
        _gateup_kernel,
        grid=(n // tm, nf),
        in_specs=[
            pl.BlockSpec((tm, d), lambda i, j: (i, 0)),
            pl.BlockSpec((d, tn), lambda i, j: (0, j)),
            pl.BlockSpec((d, tn), lambda i, j: (0, nf + j)),
        ],
        out_specs=pl.BlockSpec((tm, tn), lambda i, j: (i, j)),
        out_shape=jax.ShapeDtypeStruct((n, f), BF16),
        compiler_params=_cp(("arbitrary", "arbitrary")),
        name="ffn_gateup",
    )(h, w_in, w_in)


def _mm_kernel(x_ref, w_ref, o_ref, *, act):
    acc = jnp.dot(x_ref[...], w_ref[...].astype(BF16), preferred_element_type=F32)
    if act == "sigmoid":
        acc = jax.nn.sigmoid(acc)
    o_ref[...] = acc.astype(o_ref.dtype)


def _mm_res_kernel(x_ref, w_ref, r_ref, o_ref):
    o_ref[...] = r_ref[...] + jnp.dot(x_ref[...], w_ref[...].astype(BF16), preferred_element_type=F32)


_NT_DIMS = (((1,), (1,)), ((), ()))


def _mm_nt_kernel(x_ref, w_ref, o_ref, *, act):
    acc = lax.dot_general(x_ref[...], w_ref[...].astype(BF16), _NT_DIMS, preferred_element_type=F32)
    if act == "sigmoid":
        acc = jax.nn.sigmoid(acc)
    o_ref[...] = acc.astype(o_ref.dtype)


def _mm_nt_shift_kernel(x_ref, wa_ref, wb_ref, o_ref, w_scr, *, act):
    @pl.when(pl.program_id(1) == 0)
    def _():
        head = wa_ref.shape[0]
        w_scr[:head] = wa_ref[...].astype(BF16)
        w_scr[head:] = wb_ref[:w_scr.shape[0] - head].astype(BF16)

    acc = lax.dot_general(x_ref[...], w_scr[...], _NT_DIMS, preferred_element_type=F32)
    if act == "sigmoid":
        acc = jax.nn.sigmoid(acc)
    o_ref[...] = acc.astype(o_ref.dtype)


def _matmul_nt(x, wt, *, row0=0, rows=None, out_dtype=F32, act=None, tn=512):
    m, k = x.shape
    rows = wt.shape[0] - row0 if rows is None else rows
    tn = min(tn, _round_up(rows, V7X_LANES))
    nb = -(-rows // tn)
    tm = _tile(m, 1100)
    out_shape = jax.ShapeDtypeStruct((m, nb * tn), out_dtype)
    if row0 % tn == 0:
        jb = row0 // tn
        return pl.pallas_call(
            functools.partial(_mm_nt_kernel, act=act),
            grid=(m // tm, nb),
            in_specs=[pl.BlockSpec((tm, k), lambda i, j: (i, 0)),
                      pl.BlockSpec((tn, k), lambda i, j: (jb + j, 0))],
            out_specs=pl.BlockSpec((tm, tn), lambda i, j: (i, j)),
            out_shape=out_shape,
            compiler_params=_cp(("arbitrary", "arbitrary")),
            name="matmul_nt",
        )(x, wt)
    head = tn - row0 % tn
    assert head % 16 == 0 and row0 % head == 0 and tn % head == 0
    ja, jb, step = row0 // head, (row0 + head) // tn, tn // head
    return pl.pallas_call(
        functools.partial(_mm_nt_shift_kernel, act=act),
        grid=(nb, m // tm),
        in_specs=[pl.BlockSpec((tm, k), lambda j, i: (i, 0)),
                  pl.BlockSpec((head, k), lambda j, i: (ja + step * j, 0)),
                  pl.BlockSpec((tn, k), lambda j, i: (jb + j, 0))],
        out_specs=pl.BlockSpec((tm, tn), lambda j, i: (i, j)),
        out_shape=out_shape,
        scratch_shapes=[pltpu.VMEM((tn, k), BF16)],
        compiler_params=_cp(("arbitrary", "arbitrary")),
        name="matmul_nt_shifted",
    )(x, wt, wt)


def _matmul(x, w, *, col0=0, cols=None, out_dtype=F32, act=None, residual=None, tn=512,
            x_resident=False):
    m, k = x.shape
    cols = w.shape[1] - col0 if cols is None else cols
    tn = min(tn, _round_up(cols, V7X_LANES))
    assert col0 % tn == 0
    nb, jb = -(-cols // tn), col0 // tn
    tm = _tile(m, 1100)
    x_mode = pl.Buffered(1) if x_resident else None
    in_specs = [pl.BlockSpec((tm, k), lambda i, j: (i, 0), pipeline_mode=x_mode),
                pl.BlockSpec((k, tn), lambda i, j: (0, jb + j))]
    args = [x, w]
    if residual is None:
        kern = functools.partial(_mm_kernel, act=act)
    else:
        kern = _mm_res_kernel
        in_specs.append(pl.BlockSpec((tm, tn), lambda i, j: (i, j)))
        args.append(residual)
    return pl.pallas_call(
        kern,
        grid=(m // tm, nb),
        in_specs=in_specs,
        out_specs=pl.BlockSpec((tm, tn), lambda i, j: (i, j)),
        out_shape=jax.ShapeDtypeStruct((m, nb * tn), out_dtype),
        compiler_params=_cp(("arbitrary", "arbitrary")),
        name="matmul",
    )(*args)


def _shift_rows(x, k, tile_row0, seq_len, halo):
    rows = lax.broadcasted_iota(jnp.int32, x.shape, 0)
    out = pltpu.roll(x, k, axis=0)
    at_start = (tile_row0 % seq_len) == 0
    for q in range(k):
        h = halo[V7X_SUBLANES - k + q:V7X_SUBLANES - k + q + 1, :]
        h = jnp.where(at_start, jnp.zeros_like(h), h)
        out = jnp.where(rows == q, h, out)
    return out


def _lowrank_maps(low, param_refs, low_valid):
    _, w0_ref, a0_ref, lw_ref, la_ref, lg_ref = param_refs
    low = jnp.where(lax.broadcasted_iota(jnp.int32, low.shape, 1) < low_valid, low, 0.0)
    z = w0_ref[...] + jnp.dot(jnp.tanh(low).astype(BF16), lw_ref[...], preferred_element_type=F32)
    decay = jnp.exp(-jnp.exp(-W_LOG_SHIFT) * jax.nn.sigmoid(z))
    a = jax.nn.sigmoid(a0_ref[...] + jnp.dot(low.astype(BF16), la_ref[...], preferred_element_type=F32))
    g = jnp.dot(jax.nn.sigmoid(low).astype(BF16), lg_ref[...], preferred_element_type=F32)
    return decay, a, g


def _premix(x, prev, param_refs, d_a, low_valid):
    ps = x + (prev - x) * param_refs[0][...]
    decay, a, g = _lowrank_maps(ps[:, 3 * d_a:], param_refs, low_valid)
    return ps[:, :d_a], decay, ps[:, d_a:2 * d_a], ps[:, 2 * d_a:3 * d_a], a, g


def _rwkv_pre_first_kernel(x_ref, first_ref, *refs, d_a, low_valid, seq_len):
    x = x_ref[...]
    t = lax.broadcasted_iota(jnp.int32, x.shape, 0) % seq_len
    prev = jnp.where(t == 0, first_ref[...], pltpu.roll(x, 1, axis=0))
    for o_ref, val in zip(refs[6:], _premix(x, prev, refs[:6], d_a, low_valid)):
        o_ref[...] = val


def _rwkv_pre_first(p_a, row0, rows, seq_len, first, params, low_valid):
    pw = p_a.shape[1]
    d_a = params[1].shape[1]
    tr = _tile(rows, 128, _lcm(seq_len, V7X_SUBLANES))
    assert row0 % tr == 0 and rows % tr == 0
    off = row0 // tr
    full = lambda a: pl.BlockSpec(a.shape, lambda i: (0, 0))
    out_spec = pl.BlockSpec((tr, d_a), lambda i: (i, 0))
    return pl.pallas_call(
        functools.partial(_rwkv_pre_first_kernel, d_a=d_a, low_valid=low_valid, seq_len=seq_len),
        grid=(rows // tr,),
        in_specs=[pl.BlockSpec((tr, pw), lambda i: (off + i, 0)),
                  pl.BlockSpec((tr, pw), lambda i: (i, 0))] + [full(p) for p in params],
        out_specs=[out_spec] * 6,
        out_shape=[jax.ShapeDtypeStruct((rows, d_a), F32)] * 6,
        compiler_params=_cp(("arbitrary",)),
        name="rwkv_pre_first",
    )(p_a, first, *params)


def _chans_to_lanes(xs, n_heads):
    bg = len(xs)
    ln = bg * n_heads
    group = lax.broadcasted_iota(jnp.int32, (xs[0].shape[0], ln), 1) // n_heads
    out = []
    for q in range(HEAD_A // bg):
        chunk = [x[:, q * ln:(q + 1) * ln] for x in xs]
        rolled = [[c if s == 0 else pltpu.roll(c, s * n_heads, axis=1) for s in range(bg)] for c in chunk]
        for jl in range(bg):
            acc = rolled[bg - 1][(bg - 1 - jl) % bg]
            for b in range(bg - 2, -1, -1):
                acc = jnp.where(group == b, rolled[b][(b - jl) % bg], acc)
            out.append(acc)
    return out


def _rwkv_pre_lanes_kernel(*refs, bg, d_a, low_valid, tr, seq_len):
    x_refs, halo_refs = refs[:bg], refs[bg:2 * bg]
    param_refs = refs[2 * bg:2 * bg + 6]
    r_ref, w_ref, k_ref, a_ref, v_ref, g_ref, tail_ref = refs[2 * bg + 6:]
    n_heads = d_a // HEAD_A
    shifted = []
    for b in range(bg):
        x = x_refs[b][...]
        prev = _shift_rows(x, 1, pl.program_id(1) * tr, seq_len, halo_refs[b][...])
        shifted.append(x + (prev - x) * param_refs[0][...])
        tail_ref[b] = x[tr - V7X_SUBLANES:, :]
    low = jnp.concatenate([ps[:, 3 * d_a:] for ps in shifted], axis=0)
    decay, a, g = _lowrank_maps(low, param_refs, low_valid)
    mixed = []
    for b, ps in enumerate(shifted):
        rows = slice(b * tr, (b + 1) * tr)
        v_ref[b] = ps[:, 2 * d_a:3 * d_a]
        g_ref[b] = g[rows]
        mixed.append((ps[:, :d_a], decay[rows], ps[:, d_a:2 * d_a], a[rows]))
    for o_ref, vals in zip((r_ref, w_ref, k_ref, a_ref), zip(*mixed)):
        for j, val in enumerate(_chans_to_lanes(vals, n_heads)):
            o_ref[0, j] = val


def _rwkv_pre_lanes(p_a, row0, batch, seq_len, params, low_valid, bg):
    pw = p_a.shape[1]
    d_a = params[1].shape[1]
    ln = bg * (d_a // HEAD_A)
    tr = _tile(seq_len, 32, V7X_SUBLANES)
    assert row0 % tr == 0 and batch % bg == 0
    off, tpb, hb = row0 // tr, seq_len // tr, tr // V7X_SUBLANES
    xs = [pl.BlockSpec((tr, pw), lambda g, i, b=b: (off + (g * bg + b) * tpb + i, 0)) for b in range(bg)]
    halos = [pl.BlockSpec((V7X_SUBLANES, pw),
                          lambda g, i, b=b: (jnp.maximum((off + (g * bg + b) * tpb + i) * hb - 1, 0), 0))
             for b in range(bg)]
    full = lambda a: pl.BlockSpec(a.shape, lambda g, i: (0, 0))
    lanes = pl.BlockSpec((1, HEAD_A, tr, ln), lambda g, i: (g, 0, i, 0))
    rowsp = pl.BlockSpec((bg, tr, d_a), lambda g, i: (g, i, 0))
    lanes_shape = jax.ShapeDtypeStruct((batch // bg, HEAD_A, seq_len, ln), F32)
    rows_shape = jax.ShapeDtypeStruct((batch, seq_len, d_a), F32)
    tail = pl.BlockSpec((bg, V7X_SUBLANES, pw), lambda g, i: (g, 0, 0))
    r, w, k, a, v, g, tails = pl.pallas_call(
        functools.partial(_rwkv_pre_lanes_kernel, bg=bg, d_a=d_a, low_valid=low_valid, tr=tr,
                          seq_len=seq_len),
        grid=(batch // bg, tpb),
        in_specs=xs + halos + [full(p) for p in params],
        out_specs=[lanes] * 4 + [rowsp] * 2 + [tail],
        out_shape=[lanes_shape] * 4 + [rows_shape] * 2
        + [jax.ShapeDtypeStruct((batch, V7X_SUBLANES, pw), F32)],
        compiler_params=_cp(("arbitrary", "arbitrary")),
        name="rwkv_pre_lanes",
    )(*([p_a] * (2 * bg)), *params)
    return (r, w, k, a, v.reshape(batch * seq_len, d_a), g.reshape(batch * seq_len, d_a),
            tails[:, V7X_SUBLANES - 1])


def _scan_chunk_setup(seq_refs, par_refs, scr_refs):
    r_ref, _, k_ref, a_ref, _ = seq_refs
    kk_ref, ka_ref, rk_ref = par_refs[:3]
    av_ref, bv_ref, km_ref, bonus_ref = scr_refs
    row = lambda ref, j: ref[pl.ds(j, 1), :]
    hd = k_ref.shape[1]
    ss = jnp.zeros(bonus_ref.shape, F32)
    for j in range(hd):
        kk = k_ref[0, j] * row(kk_ref, j)
        ss = ss + kk * kk
    inv = 1.0 / jnp.maximum(jnp.sqrt(ss), 1e-12)
    bonus = jnp.zeros(bonus_ref.shape, F32)
    for j in range(hd):
        k, a = k_ref[0, j], a_ref[0, j]
        kk = (k * row(kk_ref, j)) * inv
        kmod = k * (1.0 + (a - 1.0) * row(ka_ref, j))
        av_ref[j] = -kk
        bv_ref[j] = kk * a
        km_ref[j] = kmod
        bonus = bonus + (r_ref[0, j] * kmod) * row(rk_ref, j)
    bonus_ref[...] = bonus


def _scan_step(t, seq_refs, par_refs, scr_refs, y_ref, s_ref, j_loop):
    r_ref, w_ref, _, _, v_ref = seq_refs
    lnw_ref, lnb_ref = par_refs[3:]
    av_ref, bv_ref, km_ref, bonus_ref = scr_refs
    at = pl.ds(t, 1)
    v = v_ref[0, t]
    sa = j_loop(lambda j, acc: acc + s_ref[0, j] * av_ref[j, at, :], jnp.zeros_like(v))

    def update(j, y):
        sj = (s_ref[0, j] * w_ref[0, j, at, :] + sa * bv_ref[j, at, :]) + v * km_ref[j, at, :]
        s_ref[0, j] = sj
        return y + sj * r_ref[0, j, at, :]

    y = j_loop(update, jnp.zeros_like(v))
    mu = jnp.mean(y, axis=0, keepdims=True)
    yc = y - mu
    var = jnp.mean(yc * yc, axis=0, keepdims=True)
    yn = (yc * lax.rsqrt(var + GN_EPS)) * lnw_ref[...] + lnb_ref[...]
    y_ref[0, t] = yn + bonus_ref[at, :] * v


def _blocked_j_loop(hd):
    def run(body, init):
        def block(jb, carry):
            for jj in range(SCAN_J_BLOCK):
                carry = body(jb * SCAN_J_BLOCK + jj, carry)
            return carry
        return lax.fori_loop(0, hd // SCAN_J_BLOCK, block, init)
    return run


def _unrolled_j_loop(hd):
    def run(body, init):
        carry = init
        for j in range(hd):
            carry = body(j, carry)
        return carry
    return run


def _scan_kernel(*refs, tc):
    seq_refs, par_refs, s0_ref = refs[:5], refs[5:10], refs[10]
    y_ref, s_ref = refs[11:13]
    scr_refs = refs[13:]

    @pl.when(pl.program_id(1) == 0)
    def _():
        s_ref[...] = s0_ref[...]

    _scan_chunk_setup(seq_refs, par_refs, scr_refs)
    j_loop = _blocked_j_loop(s_ref.shape[1])

    def step(t, carry):
        _scan_step(t, seq_refs, par_refs, scr_refs, y_ref, s_ref, j_loop)
        return carry

    lax.fori_loop(0, tc, step, 0)


def _scan_project_kernel(*refs, tcf, n_tiles, n_col_tiles, n_plain):
    x_ref, wa_ref, wb_ref = refs[:3]
    seq_refs, par_refs = refs[3:8], refs[8:13]
    plain_ref, gate_ref, y_ref, s_ref = refs[13:17]
    w_scr, acc_ref, scr_refs = refs[17], refs[18], refs[19:]
    step_id = pl.program_id(0)

    @pl.when(step_id == 0)
    def _():
        s_ref[...] = jnp.zeros_like(s_ref)

    head = wa_ref.shape[0]
    w_scr[:head] = wa_ref[...].astype(BF16)
    w_scr[head:] = wb_ref[:w_scr.shape[0] - head].astype(BF16)

    _scan_chunk_setup(seq_refs, par_refs, scr_refs)
    j_loop = _unrolled_j_loop(s_ref.shape[1])
    slab = x_ref.shape[0] // PROJECT_SLABS
    for t in range(tcf):
        if t % (tcf // PROJECT_SLABS) == 0:
            rows = pl.ds((t // (tcf // PROJECT_SLABS)) * slab, slab)
            acc_ref[rows, :] = lax.dot_general(x_ref[rows, :], w_scr[...], _NT_DIMS,
                                               preferred_element_type=F32)
        _scan_step(t, seq_refs, par_refs, scr_refs, y_ref, s_ref, j_loop)

    col_tile = jnp.minimum(step_id, n_tiles - 1) % n_col_tiles

    @pl.when(col_tile < n_plain)
    def _():
        plain_ref[...] = acc_ref[...]

    @pl.when(col_tile >= n_plain)
    def _():
        gate_ref[...] = jax.nn.sigmoid(acc_ref[...]).astype(gate_ref.dtype)


def _scan(r, w, k, a, v, params, s0):
    g, hd, t, ln = r.shape
    tc = _tile(t, 64, 1)
    by_chan = pl.BlockSpec((1, hd, tc, ln), lambda gi, c: (gi, 0, c, 0))
    by_time = pl.BlockSpec((1, tc, hd, ln), lambda gi, c: (gi, c, 0, 0))
    per_group = params[0].shape[0] > 1
    par = pl.BlockSpec((None, hd, ln), lambda gi, c: (gi if per_group else 0, 0, 0))
    st = pl.BlockSpec((1, hd, hd, ln), lambda gi, c: (gi, 0, 0, 0))
    return pl.pallas_call(
        functools.partial(_scan_kernel, tc=tc),
        grid=(g, t // tc),
        in_specs=[by_chan] * 4 + [by_time] + [par] * 5 + [st],
        out_specs=[by_time, st],
        out_shape=[jax.ShapeDtypeStruct(v.shape, F32), jax.ShapeDtypeStruct(s0.shape, F32)],
        scratch_shapes=[pltpu.VMEM((hd, tc, ln), F32)] * 3 + [pltpu.VMEM((tc, ln), F32)],
        compiler_params=_cp(("arbitrary", "arbitrary")),
        name="rwkv_scan",
    )(r, w, k, a, v, *params, s0)


def _scan_project_fits(m, t, groups, n_rows, tn=512):
    tiles = (m // _tile(m, 1100)) * (n_rows // tn)
    return groups == 1 and n_rows % tn == 0 and t % SCAN_FUSED_STEPS == 0 and t // SCAN_FUSED_STEPS >= tiles


def _scan_project(x, wt, row0, plain_rows, gate_rows, r, w, k, a, v, params, tn=512):
    m, kdim = x.shape
    _, hd, t, ln = r.shape
    tm = _tile(m, 1100)
    n_plain, n_gate = plain_rows // tn, gate_rows // tn
    nct = n_plain + n_gate
    tiles = (m // tm) * nct
    tcf = SCAN_FUSED_STEPS
    assert plain_rows % tn == 0 and gate_rows % tn == 0 and t // tcf >= tiles
    head = tn - row0 % tn
    assert head % 16 == 0 and row0 % head == 0 and tn % head == 0
    ja, jb, stride = row0 // head, (row0 + head) // tn, tn // head
    tile = lambda s: jnp.minimum(s, tiles - 1)
    row_t = lambda s: tile(s) // nct
    col_t = lambda s: tile(s) % nct
    by_chan = pl.BlockSpec((1, hd, tcf, ln), lambda s: (0, 0, s, 0))
    by_time = pl.BlockSpec((1, tcf, hd, ln), lambda s: (0, s, 0, 0))
    par = pl.BlockSpec((None, hd, ln), lambda s: (0, 0, 0))
    st = pl.BlockSpec((1, hd, hd, ln), lambda s: (0, 0, 0, 0))
    return pl.pallas_call(
        functools.partial(_scan_project_kernel, tcf=tcf, n_tiles=tiles, n_col_tiles=nct, n_plain=n_plain),
        grid=(t // tcf,),
        in_specs=[pl.BlockSpec((tm, kdim), lambda s: (row_t(s), 0), pipeline_mode=pl.Buffered(1)),
                  pl.BlockSpec((head, kdim), lambda s: (ja + stride * col_t(s), 0)),
                  pl.BlockSpec((tn, kdim), lambda s: (jb + col_t(s), 0))]
        + [by_chan] * 4 + [by_time] + [par] * 5,
        out_specs=[pl.BlockSpec((tm, tn), lambda s: (row_t(s), jnp.minimum(col_t(s), n_plain - 1))),
                   pl.BlockSpec((tm, tn), lambda s: (row_t(s), jnp.maximum(col_t(s) - n_plain, 0))),
                   by_time, st],
        out_shape=[jax.ShapeDtypeStruct((m, plain_rows), F32), jax.ShapeDtypeStruct((m, gate_rows), BF16),
                   jax.ShapeDtypeStruct(v.shape, F32), jax.ShapeDtypeStruct((1, hd, hd, ln), F32)],
        scratch_shapes=[pltpu.VMEM((tn, kdim), BF16), pltpu.VMEM((tm, tn), F32)]
        + [pltpu.VMEM((hd, tcf, ln), F32)] * 3 + [pltpu.VMEM((tcf, ln), F32)],
        compiler_params=_cp(("arbitrary",)),
        name="rwkv_scan_project",
    )(x, wt, wt, r, w, k, a, v, *params)


class _Lanes:
    def __init__(self, b, n_heads):
        self.h = n_heads
        self.by_batch = b % V7X_LANES == 0
        self.bi = V7X_LANES if self.by_batch else max(1, min(b, V7X_LANES // n_heads))
        self.bo = b // self.bi
        self.groups = self.bo * n_heads if self.by_batch else self.bo
        self.lanes = self.bi if self.by_batch else self.bi * n_heads

    def rows_by_time(self, x2d, t):
        x = x2d.reshape(self.bo, self.bi, t, self.h, HEAD_A)
        x = x.transpose(0, 3, 2, 4, 1) if self.by_batch else x.transpose(0, 2, 4, 1, 3)
        return x.reshape(self.groups, t, HEAD_A, self.lanes)

    def rows_by_chan(self, x2d, t):
        x = x2d.reshape(self.bo, self.bi, t, HEAD_A, self.h)
        x = x.transpose(0, 4, 3, 2, 1) if self.by_batch else x.transpose(0, 3, 2, 1, 4)
        return x.reshape(self.groups, HEAD_A, t, self.lanes)

    def rows_back(self, y, t):
        if self.by_batch:
            y = y.reshape(self.bo, self.h, t, HEAD_A, self.bi).transpose(0, 4, 2, 1, 3)
        else:
            y = y.reshape(self.bo, t, HEAD_A, self.bi, self.h).transpose(0, 3, 1, 4, 2)
        return y.reshape(self.bo * self.bi * t, self.h * HEAD_A)

    def param(self, p):
        p = p.reshape(self.h, HEAD_A)
        if self.by_batch:
            p = jnp.broadcast_to(p[None, :, :, None], (self.bo, self.h, HEAD_A, self.bi))
            return p.reshape(self.groups, HEAD_A, self.lanes)
        return jnp.tile(p.T, (1, self.bi))[None]

    def state_in(self, s):
        s = s.astype(F32).reshape(self.bo, self.bi, self.h, HEAD_A, HEAD_A)
        s = s.transpose(0, 2, 4, 3, 1) if self.by_batch else s.transpose(0, 4, 3, 1, 2)
        return s.reshape(self.groups, HEAD_A, HEAD_A, self.lanes)

    def state_out(self, s):
        if self.by_batch:
            s = s.reshape(self.bo, self.h, HEAD_A, HEAD_A, self.bi).transpose(0, 4, 1, 3, 2)
        else:
            s = s.reshape(self.bo, HEAD_A, HEAD_A, self.bi, self.h).transpose(0, 3, 4, 2, 1)
        return s.reshape(self.bo * self.bi, self.h, HEAD_A, HEAD_A)


def _rwkv_group(lay, rwka, v, t, s_prev, scan_params):
    params = [lay.param(p) for p in scan_params]
    if s_prev is None:
        s0 = jnp.zeros((lay.groups, HEAD_A, HEAD_A, lay.lanes), F32)
    else:
        s0 = lay.state_in(s_prev)
    y, s_new = _scan(*rwka, lay.rows_by_time(v, t), params, s0)
    return lay.rows_back(y, t), lay.state_out(s_new)


def _gate_kernel(*refs):
    y_ref, g_ref, o_ref = refs[-3:]
    o_ref[...] = (y_ref[...] * g_ref[...]).astype(o_ref.dtype)


def _gate_rows(y, g, row0, prev_out):
    rows, d_a = y.shape
    tr = _tile(_gcd(row0, rows) if row0 else rows, 512, 16)
    off = row0 // tr
    row = pl.BlockSpec((tr, d_a), lambda i: (i, 0))
    return pl.pallas_call(
        _gate_kernel,
        grid=(rows // tr,),
        in_specs=[pl.BlockSpec(memory_space=pl.ANY), row, row],
        out_specs=pl.BlockSpec((tr, d_a), lambda i: (off + i, 0)),
        out_shape=jax.ShapeDtypeStruct(prev_out.shape, BF16),
        input_output_aliases={0: 0},
        compiler_params=_cp(("arbitrary",)),
        name="rwkv_gate",
    )(prev_out, y, g)


def _conv_kernel(*refs, tr, seq_len, mode):
    gb_ref, gc_ref, u_ref, aux1_ref, aux2_ref, cw_ref = refs[-8:-2]
    y_ref, u2_ref = refs[-2:]
    f1_ref, f2_ref = aux1_ref, aux2_ref
    u2 = gc_ref[...] * u_ref[...]
    if mode == "halo":
        halo = aux1_ref[...] * aux2_ref[...]
        row0 = pl.program_id(0) * tr
        p1 = _shift_rows(u2, 1, row0, seq_len, halo)
        p2 = _shift_rows(u2, 2, row0, seq_len, halo)
    else:
        t = lax.broadcasted_iota(jnp.int32, u2.shape, 0) % seq_len
        p1 = jnp.where(t < 1, f1_ref[...], pltpu.roll(u2, 1, axis=0))
        p2 = jnp.where(t < 2, f2_ref[...], pltpu.roll(u2, 2, axis=0))
    z = p2 * cw_ref[0:1, :] + p1 * cw_ref[1:2, :]
    z = z + u2 * cw_ref[2:3, :]
    y_ref[...] = (gb_ref[...] * z).astype(y_ref.dtype)
    u2_ref[...] = u2


def _conv(p_bc, row0, rows, seq_len, firsts, conv_w, prev_out):
    n = p_bc.shape[0]
    d_b = conv_w.shape[1]
    mode = "halo" if firsts is None else "first"
    if mode == "halo":
        tr = _tile(_gcd(seq_len, rows), 256, 16)
    else:
        tr = _tile(rows, 256, _lcm(seq_len, 16))
    assert row0 % tr == 0 and rows % tr == 0 and seq_len >= CONV_W - 1
    off = row0 // tr
    col = lambda c: pl.BlockSpec((tr, d_b), lambda i: (off + i, c))
    in_specs = [col(0), col(1), col(2)]
    args = [p_bc, p_bc, p_bc]
    if mode == "halo":
        hb = tr // V7X_SUBLANES
        hcol = lambda c: pl.BlockSpec(
            (V7X_SUBLANES, d_b), lambda i: (jnp.maximum((off + i) * hb - 1, 0), c))
        in_specs += [hcol(1), hcol(2)]
        args += [p_bc, p_bc]
    else:
        in_specs += [pl.BlockSpec((tr, d_b), lambda i: (i, 0))] * 2
        args += list(firsts)
    in_specs.append(pl.BlockSpec(conv_w.shape, lambda i: (0, 0)))
    args.append(conv_w)
    in_specs = [pl.BlockSpec(memory_space=pl.ANY)] * 2 + in_specs
    args = list(prev_out) + args
    out_spec = pl.BlockSpec((tr, d_b), lambda i: (off + i, 0))
    return pl.pallas_call(
        functools.partial(_conv_kernel, tr=tr, seq_len=seq_len, mode=mode),
        grid=(rows // tr,),
        in_specs=in_specs,
        out_specs=[out_spec, out_spec],
        out_shape=[jax.ShapeDtypeStruct((n, d_b), BF16), jax.ShapeDtypeStruct((n, d_b), F32)],
        input_output_aliases={0: 0, 1: 1},
        compiler_params=_cp(("arbitrary",)),
        name="conv_" + mode,
    )(*args)


def _attn_kernel(*refs, bb, tq, heads):
    q_ref, k_ref, v_ref, o_ref = refs[-4:]
    dh = q_ref.shape[1] // heads
    scale = dh ** -0.5
    row_batch = lax.broadcasted_iota(jnp.int32, (bb * tq, dh), 0) // tq
    for h in range(heads):
        cols = slice(h * dh, (h + 1) * dh)
        q = q_ref[:, cols].astype(BF16)
        out = None
        for b in range(bb):
            k = k_ref[b, :, cols].astype(BF16)
            v = v_ref[b, :, cols].astype(BF16)
            s = lax.dot_general(q, k, (((1,), (1,)), ((), ())), preferred_element_type=F32) * scale
            p = jnp.exp(s - jnp.max(s, axis=-1, keepdims=True))
            p = p / jnp.sum(p, axis=-1, keepdims=True)
            o = jnp.dot(p.astype(BF16), v, preferred_element_type=F32)
            out = o if out is None else jnp.where(row_batch == b, o, out)
        o_ref[:, cols] = out.astype(o_ref.dtype)


def _attention(q_src, q_col, row0, b, t, mem_k, mem_v, heads, prev_out):
    n_mem, d_c = mem_k.shape[1:]
    if t >= 128:
        bb, tq = 1, _tile(t, 512, 16)
    else:
        kv_block_cap = max(1, ATTN_KV_BLOCK_BYTES // (n_mem * d_c * 4))
        bb, tq = _tile(b, min(kv_block_cap, max(1, 64 // t)), 1), t
    assert (bb * tq) % 16 == 0 and row0 % (bb * tq) == 0
    tpb = t // tq
    off = row0 // (bb * tq)
    qmap = lambda i, j: (off + i * tpb + j, q_col)
    omap = lambda i, j: (off + i * tpb + j, 0)
    kv = pl.BlockSpec((bb, n_mem, d_c), lambda i, j: (i, 0, 0))
    return pl.pallas_call(
        functools.partial(_attn_kernel, bb=bb, tq=tq, heads=heads),
        grid=(b // bb, tpb),
        in_specs=[pl.BlockSpec(memory_space=pl.ANY), pl.BlockSpec((bb * tq, d_c), qmap), kv, kv],
        out_specs=pl.BlockSpec((bb * tq, d_c), omap),
        out_shape=jax.ShapeDtypeStruct(prev_out.shape, BF16),
        input_output_aliases={0: 0},
        compiler_params=_cp(("arbitrary", "arbitrary")),
        name="mem_attention",
    )(prev_out, q_src, mem_k, mem_v)


def _merge_kernel(ya_ref, yb_ref, yc_ref, ga_ref, gb_ref, gc_ref, wa_ref, wb_ref, wc_ref, o_ref):
    def branch(y_ref, g_ref, w_ref):
        return g_ref[...].astype(F32) * jnp.dot(
            y_ref[...], w_ref[...].astype(BF16), preferred_element_type=F32)

    acc = branch(ya_ref, ga_ref, wa_ref) + branch(yb_ref, gb_ref, wb_ref)
    o_ref[...] = (acc + branch(yc_ref, gc_ref, wc_ref)).astype(o_ref.dtype)


def _merge(ya, yb, yc, gates, wa, wb, wc):
    n, d = ya.shape[0], wa.shape[1]
    tm = _tile(n, 1100)
    tn = _tile(d, 512, V7X_LANES)
    nj = d // tn
    act = lambda y: pl.BlockSpec((tm, y.shape[1]), lambda i, j: (i, 0))
    gate = lambda c: pl.BlockSpec((tm, tn), lambda i, j: (i, c * nj + j))
    wsp = lambda w: pl.BlockSpec((w.shape[0], tn), lambda i, j: (0, j))
    return pl.pallas_call(
        _merge_kernel,
        grid=(n // tm, nj),
        in_specs=[act(ya), act(yb), act(yc), gate(0), gate(1), gate(2), wsp(wa), wsp(wb), wsp(wc)],
        out_specs=pl.BlockSpec((tm, tn), lambda i, j: (i, j)),
        out_shape=jax.ShapeDtypeStruct((n, d), BF16),
        compiler_params=_cp(("arbitrary", "arbitrary")),
        name="branch_merge",
    )(ya, yb, yc, gates, gates, gates, wa, wb, wc)


def _swap_head_chan(p, n_heads, to_chan_major, axis=-1):
    axis = axis % p.ndim
    blocks = p.shape[axis] // (n_heads * HEAD_A)
    inner = (n_heads, HEAD_A) if to_chan_major else (HEAD_A, n_heads)
    q = p.reshape(p.shape[:axis] + (blocks,) + inner + p.shape[axis + 1:])
    return jnp.swapaxes(q, axis + 1, axis + 2).reshape(p.shape)


def _pad_rows(w, row0, total):
    return jnp.zeros((total, w.shape[1]), BF16).at[row0:row0 + w.shape[0]].set(w.astype(BF16))


def _first_rows(state, seq_len, width):
    b, k, c = state.shape
    out = jnp.zeros((b, seq_len, width), F32).at[:, :k, :c].set(state.astype(F32))
    return out.reshape(b * seq_len, width)


def _layer(x, h1, groups, lp, mem_kv, heads_c):
    n, d = x.shape
    d_a = lp["w0"].shape[0]
    d_b = lp["conv_w"].shape[1]
    p_a_w = lp["mu_shift"].shape[0]
    lora_w, lora_a, lora_g = (lp[q].shape[0] for q in ("w_lora_up", "a_lora_up", "g_lora_up"))
    d_c = lp["w_br_c"].shape[0]
    assert d_c == d_b and p_a_w == 3 * d_a + lora_w + lora_a + lora_g

    x = _matmul(_gateup(h1, lp["w_ffn1_in"]), lp["w_ffn1_out"], residual=x, tn=256, x_resident=True)
    h = _norm(x, lp["g_mix"])

    w_t = jnp.swapaxes(lp["w_in"], 0, 1)
    pw = _round_up(p_a_w, 512)
    w_a = jnp.concatenate([_swap_head_chan(w_t[:2 * d_a], d_a // HEAD_A, to_chan_major=True, axis=0),
                           w_t[2 * d_a:pw]], axis=0)
    p_a = _matmul_nt(h, w_a)
    p_bc = gates = None

    pw = p_a.shape[1]
    low_w = pw - 3 * d_a
    n_heads = d_a // HEAD_A
    rk_swap = lambda p: jnp.concatenate(
        [_swap_head_chan(p[..., :2 * d_a], n_heads, to_chan_major=True), p[..., 2 * d_a:]], axis=-1)
    to_cm = lambda p: _swap_head_chan(p, n_heads, to_chan_major=True)
    pre_params = (
        jnp.pad(rk_swap(lp["mu_shift"]), (0, pw - p_a_w)).reshape(1, pw),
        to_cm(lp["w0"]).reshape(1, d_a), to_cm(lp["a0"]).reshape(1, d_a),
        _pad_rows(to_cm(lp["w_lora_up"]), 0, low_w),
        _pad_rows(to_cm(lp["a_lora_up"]), lora_w, low_w),
        _pad_rows(lp["g_lora_up"], lora_w + lora_a, low_w),
    )
    scan_params = (lp["k_k"], lp["k_a"], lp["r_k"].reshape(-1), lp["lnx_w"], lp["lnx_b"])
    ya, new_states = jnp.zeros((n, d_a), BF16), []
    for row0, b, t, st in groups:
        lay = _Lanes(b, n_heads)
        if st is None:
            assert not lay.by_batch
            *rwka, v, g, last = _rwkv_pre_lanes(p_a, row0, b, t, pre_params, p_a_w - 3 * d_a, lay.bi)
            last = last[:, :p_a_w]
        else:
            first = _first_rows(rk_swap(st["shift"])[:, None, :], t, pw)
            r, w, k, v, a, g = _rwkv_pre_first(p_a, row0, b * t, t, first, pre_params, p_a_w - 3 * d_a)
            rwka = [lay.rows_by_chan(q, t) for q in (r, w, k, a)]
            last = p_a[row0 + t - 1:row0 + b * t:t, :p_a_w]
        rest_rows = (3 * d_b + d_c, 3 * d)
        if p_bc is None and st is None and _scan_project_fits(n, t, lay.groups, sum(rest_rows)):
            p_bc, gates, y, s_new = _scan_project(
                h, w_t, p_a_w, *rest_rows, *rwka, lay.rows_by_time(v, t),
                [lay.param(p) for p in scan_params])
            y, s_new = lay.rows_back(y, t), lay.state_out(s_new)
        else:
            y, s_new = _rwkv_group(lay, rwka, v, t, None if st is None else st["rwkv"], scan_params)
        ya = _gate_rows(y, g, row0, ya)
        shift_new = jnp.concatenate(
            [_swap_head_chan(last[:, :2 * d_a], n_heads, to_chan_major=False), last[:, 2 * d_a:]], axis=1)
        new_states.append({"rwkv": s_new, "shift": shift_new})

    if p_bc is None:
        p_bc = _matmul_nt(h, w_t, row0=p_a_w, rows=3 * d_b + d_c)
        gates = _matmul_nt(h, w_t, row0=p_a_w + 3 * d_b + d_c, rows=3 * d, out_dtype=BF16, act="sigmoid")

    conv = (jnp.zeros((n, d_b), BF16), jnp.zeros((n, d_b), F32))
    for row0, b, t, st in groups:
        firsts = None
        if st is not None:
            buf = st["conv"]
            firsts = (_first_rows(buf[:, 1:, :], t, d_b), _first_rows(buf, t, d_b))
        conv = _conv(p_bc, row0, b * t, t, firsts, lp["conv_w"], conv)
    yb, u2 = conv
    for (row0, b, t, st), ns in zip(groups, new_states):
        tail = [u2[row0 + t - q:row0 + b * t:t] for q in range(CONV_W - 1, 0, -1)]
        ns["conv"] = jnp.stack(tail, axis=1)

    yc = jnp.zeros((n, d_c), BF16)
    for (row0, b, t, st), (mk, mv) in zip(groups, mem_kv):
        yc = _attention(p_bc, (3 * d_b) // d_c, row0, b, t,
                        mk.reshape(b, -1, d_c), mv.reshape(b, -1, d_c), heads_c, yc)

    merged = _merge(ya, yb, yc, gates, lp["w_br_a"], lp["w_br_b"], lp["w_br_c"])
    x = _matmul(merged, lp["w_out"], residual=x)

    h = _norm(x, lp["g_ffn2"])
    x = _matmul(_gateup(h, lp["w_ffn2_in"]), lp["w_ffn2_out"], residual=x, tn=256, x_resident=True)
    return x, new_states


def kernel(x_prompt, x_sample, mem_prompt, cache_mem_k, cache_mem_v, state_rwkv, state_shift, state_conv, g_ffn1, w_ffn1_in, w_ffn1_out, g_mix, w_in, mu_shift, w0, w_lora_up, a0, a_lora_up, g_lora_up, k_k, k_a, r_k, lnx_w, lnx_b, conv_w, g_mem, w_mem_kv, w_br_a, w_br_b, w_br_c, w_out, g_ffn2, w_ffn2_in, w_ffn2_out, g_final):
    bp, tp, d = x_prompt.shape
    bs, ts, _ = x_sample.shape
    n_p, n_s = bp * tp, bs * ts
    depth = w_in.shape[0]
    n_mem, heads_c, dh_c = cache_mem_k.shape[2:]
    d_c = heads_c * dh_c

    per_layer = dict(
        g_ffn1=g_ffn1, w_ffn1_in=w_ffn1_in, w_ffn1_out=w_ffn1_out, g_mix=g_mix, w_in=w_in,
        mu_shift=mu_shift, w0=w0, w_lora_up=w_lora_up, a0=a0, a_lora_up=a_lora_up,
        g_lora_up=g_lora_up, k_k=k_k, k_a=k_a, r_k=r_k, lnx_w=lnx_w, lnx_b=lnx_b, conv_w=conv_w,
        w_br_a=w_br_a, w_br_b=w_br_b, w_br_c=w_br_c, w_out=w_out, g_ffn2=g_ffn2,
        w_ffn2_in=w_ffn2_in, w_ffn2_out=w_ffn2_out)

    mem_rows = mem_prompt.reshape(bp * n_mem, d)
    outs = {q: [] for q in ("mk", "mv", "rp", "shp", "cp", "rs", "shs", "cs")}
    x = None
    for l in range(depth):
        lp = {q: p[l] for q, p in per_layer.items()}
        if l == 0:
            x, h1 = _norm_concat(x_prompt.reshape(n_p, d), x_sample.reshape(n_s, d), lp["g_ffn1"])
        else:
            h1 = _norm(x, lp["g_ffn1"])
        kv = _matmul(_norm(mem_rows, g_mem[l]), w_mem_kv[l])
        mk_p = kv[:, :d_c].reshape(bp, n_mem, heads_c, dh_c)
        mv_p = kv[:, d_c:2 * d_c].reshape(bp, n_mem, heads_c, dh_c)
        groups = [
            (0, bp, tp, None),
            (n_p, bs, ts, {"rwkv": state_rwkv[l], "shift": state_shift[l], "conv": state_conv[l]}),
        ]
        prompt_kv = (kv[:, :d_c].reshape(bp, n_mem, d_c), kv[:, d_c:2 * d_c].reshape(bp, n_mem, d_c))
        x, (st_p, st_s) = _layer(
            x, h1, groups, lp, [prompt_kv, (cache_mem_k[l], cache_mem_v[l])], heads_c)
        outs["mk"].append(mk_p)
        outs["mv"].append(mv_p)
        for tag, st in (("p", st_p), ("s", st_s)):
            outs["r" + tag].append(st["rwkv"])
            outs["sh" + tag].append(st["shift"])
            outs["c" + tag].append(st["conv"])
    y_prompt = _norm(x, g_final, F32, 0, n_p).reshape(bp, tp, d)
    y_sample = _norm(x, g_final, F32, n_p, n_s).reshape(bs, ts, d)
    stack = lambda q: jnp.stack(outs[q], 0)
    return (y_prompt, y_sample, stack("mk"), stack("mv"), stack("rp"), stack("shp"), stack("cp"),
            stack("rs"), stack("shs"), stack("cs"))
```

```python
import functools

import jax
import jax.numpy as jnp
from jax import lax
from jax.experimental import pallas as pl
from jax.experimental.pallas import tpu as pltpu

F32 = jnp.float32
BF16 = jnp.bfloat16

EPS = 1e-6
GN_EPS = 64e-5
HEAD_A = 64
CONV_W = 3
V7X_LANES = 128
V7X_SUBLANES = 8
V7X_VMEM_LIMIT = 56 * 1024 * 1024
W_LOG_SHIFT = 0.5
ATTN_KV_BLOCK_BYTES = 4 * 1024 * 1024
SCAN_J_BLOCK = 8
SCAN_FUSED_STEPS = 8
PROJECT_SLABS = 4


def _cp(sem, vmem=V7X_VMEM_LIMIT):
    return pltpu.CompilerParams(dimension_semantics=sem, vmem_limit_bytes=vmem)


def _tile(n, cap, mult=16):
    best = None
    for t in range(mult, min(n, cap) + 1, mult):
        if n % t == 0:
            best = t
    return best if best is not None else n


def _round_up(n, m):
    return (n + m - 1) // m * m


def _rms(x, g):
    ms = jnp.mean(x * x, axis=-1, keepdims=True)
    return (x * lax.rsqrt(ms + EPS)) * g


def _norm2_kernel(xp_ref, xs_ref, g_ref, x_ref, h_ref, *, n_prompt_tiles):
    i = pl.program_id(0)

    def emit(x):
        x_ref[...] = x
        h_ref[...] = _rms(x, g_ref[...]).astype(BF16)

    @pl.when(i < n_prompt_tiles)
    def _():
        emit(xp_ref[...])

    @pl.when(i >= n_prompt_tiles)
    def _():
        emit(xs_ref[...])


def _norm_concat(xp, xs, g):
    (n_p, d), n_s = xp.shape, xs.shape[0]
    tr = _tile(_gcd(n_p, n_s), 256, 8)
    npt, nst = n_p // tr, n_s // tr
    n = n_p + n_s
    return pl.pallas_call(
        functools.partial(_norm2_kernel, n_prompt_tiles=npt),
        grid=(npt + nst,),
        in_specs=[
            pl.BlockSpec((tr, d), lambda i: (jnp.minimum(i, npt - 1), 0)),
            pl.BlockSpec((tr, d), lambda i: (jnp.maximum(i - npt, 0), 0)),
            pl.BlockSpec((1, d), lambda i: (0, 0)),
        ],
        out_specs=[pl.BlockSpec((tr, d), lambda i: (i, 0)),
                   pl.BlockSpec((tr, d), lambda i: (i, 0))],
        out_shape=[jax.ShapeDtypeStruct((n, d), F32), jax.ShapeDtypeStruct((n, d), BF16)],
        compiler_params=_cp(("arbitrary",)),
        name="norm_concat",
    )(xp, xs, g.reshape(1, d))


def _gcd(a, b):
    while b:
        a, b = b, a % b
    return a


def _lcm(a, b):
    return a * b // _gcd(a, b)


def _norm_kernel(x_ref, g_ref, h_ref):
    h_ref[...] = _rms(x_ref[...], g_ref[...]).astype(h_ref.dtype)


def _norm(x, g, out_dtype=BF16, row0=0, rows=None):
    n, d = x.shape
    rows = n if rows is None else rows
    tr = _tile(_gcd(row0, rows) if row0 else rows, 256, 16)
    off = row0 // tr
    return pl.pallas_call(
        _norm_kernel,
        grid=(rows // tr,),
        in_specs=[pl.BlockSpec((tr, d), lambda i: (off + i, 0)),
                  pl.BlockSpec((1, d), lambda i: (0, 0))],
        out_specs=pl.BlockSpec((tr, d), lambda i: (i, 0)),
        out_shape=jax.ShapeDtypeStruct((rows, d), out_dtype),
        compiler_params=_cp(("arbitrary",)),
        name="norm",
    )(x, g.reshape(1, d))


def _gateup_kernel(h_ref, wg_ref, wu_ref, o_ref):
    h = h_ref[...]
    gate = jnp.dot(h, wg_ref[...].astype(BF16), preferred_element_type=F32)
    up = jnp.dot(h, wu_ref[...].astype(BF16), preferred_element_type=F32)
    o_ref[...] = ((gate * jax.nn.sigmoid(gate)) * up * 0.5).astype(o_ref.dtype)


def _gateup(h, w_in):
    n, d = h.shape
    f = w_in.shape[1] // 2
    tm = _tile(n, 1100)
    tn = _tile(f, 256, V7X_LANES)
    nf = f // tn
    return pl.pallas_call(
        _gateup_kernel,
        grid=(n // tm, nf),
        in_specs=[
            pl.BlockSpec((tm, d), lambda i, j: (i, 0)),
            pl.BlockSpec((d, tn), lambda i, j: (0, j)),
            pl.BlockSpec((d, tn), lambda i, j: (0, nf + j)),
        ],
        out_specs=pl.BlockSpec((tm, tn), lambda i, j: (i, j)),
        out_shape=jax.ShapeDtypeStruct((n, f), BF16),
        compiler_params=_cp(("arbitrary", "arbitrary")),
        name="ffn_gateup",
    )(h, w_in, w_in)


def _mm_kernel(x_ref, w_ref, o_ref, *, act):
    acc = jnp.dot(x_ref[...], w_ref[...].astype(BF16), preferred_element_type=F32)
    if act == "sigmoid":
        acc = jax.nn.sigmoid(acc)
    o_ref[...] = acc.astype(o_ref.dtype)


def _mm_res_kernel(x_ref, w_ref, r_ref, o_ref):
    o_ref[...] = r_ref[...] + jnp.dot(x_ref[...], w_ref[...].astype(BF16), preferred_element_type=F32)


_NT_DIMS = (((1,), (1,)), ((), ()))


def _mm_nt_kernel(x_ref, w_ref, o_ref, *, act):
    acc = lax.dot_general(x_ref[...], w_ref[...].astype(BF16), _NT_DIMS, preferred_element_type=F32)
    if act == "sigmoid":
        acc = jax.nn.sigmoid(acc)
    o_ref[...] = acc.astype(o_ref.dtype)


def _mm_nt_shift_kernel(x_ref, wa_ref, wb_ref, o_ref, w_scr, *, act):
    @pl.when(pl.program_id(1) == 0)
    def _():
        head = wa_ref.shape[0]
        w_scr[:head] = wa_ref[...].astype(BF16)
        w_scr[head:] = wb_ref[:w_scr.shape[0] - head].astype(BF16)

    acc = lax.dot_general(x_ref[...], w_scr[...], _NT_DIMS, preferred_element_type=F32)
    if act == "sigmoid":
        acc = jax.nn.sigmoid(acc)
    o_ref[...] = acc.astype(o_ref.dtype)


def _matmul_nt(x, wt, *, row0=0, rows=None, out_dtype=F32, act=None, tn=512):
    m, k = x.shape
    rows = wt.shape[0] - row0 if rows is None else rows
    tn = min(tn, _round_up(rows, V7X_LANES))
    nb = -(-rows // tn)
    tm = _tile(m, 1100)
    out_shape = jax.ShapeDtypeStruct((m, nb * tn), out_dtype)
    if row0 % tn == 0:
        jb = row0 // tn
        return pl.pallas_call(
            functools.partial(_mm_nt_kernel, act=act),
            grid=(m // tm, nb),
            in_specs=[pl.BlockSpec((tm, k), lambda i, j: (i, 0)),
                      pl.BlockSpec((tn, k), lambda i, j: (jb + j, 0))],
            out_specs=pl.BlockSpec((tm, tn), lambda i, j: (i, j)),
            out_shape=out_shape,
            compiler_params=_cp(("arbitrary", "arbitrary")),
            name="matmul_nt",
        )(x, wt)
    head = tn - row0 % tn
    assert head % 16 == 0 and row0 % head == 0 and tn % head == 0
    ja, jb, step = row0 // head, (row0 + head) // tn, tn // head
    return pl.pallas_call(
        functools.partial(_mm_nt_shift_kernel, act=act),
        grid=(nb, m // tm),
        in_specs=[pl.BlockSpec((tm, k), lambda j, i: (i, 0)),
                  pl.BlockSpec((head, k), lambda j, i: (ja + step * j, 0)),
                  pl.BlockSpec((tn, k), lambda j, i: (jb + j, 0))],
        out_specs=pl.BlockSpec((tm, tn), lambda j, i: (i, j)),
        out_shape=out_shape,
        scratch_shapes=[pltpu.VMEM((tn, k), BF16)],
        compiler_params=_cp(("arbitrary", "arbitrary")),
        name="matmul_nt_shifted",
    )(x, wt, wt)


def _matmul(x, w, *, col0=0, cols=None, out_dtype=F32, act=None, residual=None, tn=512,
            x_resident=False):
    m, k = x.shape
    cols = w.shape[1] - col0 if cols is None else cols
    tn = min(tn, _round_up(cols, V7X_LANES))
    assert col0 % tn == 0
    nb, jb = -(-cols // tn), col0 // tn
    tm = _tile(m, 1100)
    x_mode = pl.Buffered(1) if x_resident else None
    in_specs = [pl.BlockSpec((tm, k), lambda i, j: (i, 0), pipeline_mode=x_mode),
                pl.BlockSpec((k, tn), lambda i, j: (0, jb + j))]
    args = [x, w]
    if residual is None:
        kern = functools.partial(_mm_kernel, act=act)
    else:
        kern = _mm_res_kernel
        in_specs.append(pl.BlockSpec((tm, tn), lambda i, j: (i, j)))
        args.append(residual)
    return pl.pallas_call(
        kern,
        grid=(m // tm, nb),
        in_specs=in_specs,
        out_specs=pl.BlockSpec((tm, tn), lambda i, j: (i, j)),
        out_shape=jax.ShapeDtypeStruct((m, nb * tn), out_dtype),
        compiler_params=_cp(("arbitrary", "arbitrary")),
        name="matmul",
    )(*args)


def _shift_rows(x, k, tile_row0, seq_len, halo):
    rows = lax.broadcasted_iota(jnp.int32, x.shape, 0)
    out = pltpu.roll(x, k, axis=0)
    at_start = (tile_row0 % seq_len) == 0
    for q in range(k):
        h = halo[V7X_SUBLANES - k + q:V7X_SUBLANES - k + q + 1, :]
        h = jnp.where(at_start, jnp.zeros_like(h), h)
        out = jnp.where(rows == q, h, out)
    return out


def _lowrank_maps(low, param_refs, low_valid):
    _, w0_ref, a0_ref, lw_ref, la_ref, lg_ref = param_refs
    low = jnp.where(lax.broadcasted_iota(jnp.int32, low.shape, 1) < low_valid, low, 0.0)
    z = w0_ref[...] + jnp.dot(jnp.tanh(low).astype(BF16), lw_ref[...], preferred_element_type=F32)
    decay = jnp.exp(-jnp.exp(-W_LOG_SHIFT) * jax.nn.sigmoid(z))
    a = jax.nn.sigmoid(a0_ref[...] + jnp.dot(low.astype(BF16), la_ref[...], preferred_element_type=F32))
    g = jnp.dot(jax.nn.sigmoid(low).astype(BF16), lg_ref[...], preferred_element_type=F32)
    return decay, a, g


def _premix(x, prev, param_refs, d_a, low_valid):
    ps = x + (prev - x) * param_refs[0][...]
    decay, a, g = _lowrank_maps(ps[:, 3 * d_a:], param_refs, low_valid)
    return ps[:, :d_a], decay, ps[:, d_a:2 * d_a], ps[:, 2 * d_a:3 * d_a], a, g


def _rwkv_pre_first_kernel(x_ref, first_ref, *refs, d_a, low_valid, seq_len):
    x = x_ref[...]
    t = lax.broadcasted_iota(jnp.int32, x.shape, 0) % seq_len
    prev = jnp.where(t == 0, first_ref[...], pltpu.roll(x, 1, axis=0))
    for o_ref, val in zip(refs[6:], _premix(x, prev, refs[:6], d_a, low_valid)):
        o_ref[...] = val


def _rwkv_pre_first(p_a, row0, rows, seq_len, first, params, low_valid):
    pw = p_a.shape[1]
    d_a = params[1].shape[1]
    tr = _tile(rows, 128, _lcm(seq_len, V7X_SUBLANES))
    assert row0 % tr == 0 and rows % tr == 0
    off = row0 // tr
    full = lambda a: pl.BlockSpec(a.shape, lambda i: (0, 0))
    out_spec = pl.BlockSpec((tr, d_a), lambda i: (i, 0))
    return pl.pallas_call(
        functools.partial(_rwkv_pre_first_kernel, d_a=d_a, low_valid=low_valid, seq_len=seq_len),
        grid=(rows // tr,),
        in_specs=[pl.BlockSpec((tr, pw), lambda i: (off + i, 0)),
                  pl.BlockSpec((tr, pw), lambda i: (i, 0))] + [full(p) for p in params],
        out_specs=[out_spec] * 6,
        out_shape=[jax.ShapeDtypeStruct((rows, d_a), F32)] * 6,
        compiler_params=_cp(("arbitrary",)),
        name="rwkv_pre_first",
    )(p_a, first, *params)


def _chans_to_lanes(xs, n_heads):
    bg = len(xs)
    ln = bg * n_heads
    group = lax.broadcasted_iota(jnp.int32, (xs[0].shape[0], ln), 1) // n_heads
    out = []
    for q in range(HEAD_A // bg):
        chunk = [x[:, q * ln:(q + 1) * ln] for x in xs]
        rolled = [[c if s == 0 else pltpu.roll(c, s * n_heads, axis=1) for s in range(bg)] for c in chunk]
        for jl in range(bg):
            acc = rolled[bg - 1][(bg - 1 - jl) % bg]
            for b in range(bg - 2, -1, -1):
                acc = jnp.where(group == b, rolled[b][(b - jl) % bg], acc)
            out.append(acc)
    return out


def _rwkv_pre_lanes_kernel(*refs, bg, d_a, low_valid, tr, seq_len):
    x_refs, halo_refs = refs[:bg], refs[bg:2 * bg]
    param_refs = refs[2 * bg:2 * bg + 6]
    r_ref, w_ref, k_ref, a_ref, v_ref, g_ref, tail_ref = refs[2 * bg + 6:]
    n_heads = d_a // HEAD_A
    shifted = []
    for b in range(bg):
        x = x_refs[b][...]
        prev = _shift_rows(x, 1, pl.program_id(1) * tr, seq_len, halo_refs[b][...])
        shifted.append(x + (prev - x) * param_refs[0][...])
        tail_ref[b] = x[tr - V7X_SUBLANES:, :]
    low = jnp.concatenate([ps[:, 3 * d_a:] for ps in shifted], axis=0)
    decay, a, g = _lowrank_maps(low, param_refs, low_valid)
    mixed = []
    for b, ps in enumerate(shifted):
        rows = slice(b * tr, (b + 1) * tr)
        v_ref[b] = ps[:, 2 * d_a:3 * d_a]
        g_ref[b] = g[rows]
        mixed.append((ps[:, :d_a], decay[rows], ps[:, d_a:2 * d_a], a[rows]))
    for o_ref, vals in zip((r_ref, w_ref, k_ref, a_ref), zip(*mixed)):
        for j, val in enumerate(_chans_to_lanes(vals, n_heads)):
            o_ref[0, j] = val


def _rwkv_pre_lanes(p_a, row0, batch, seq_len, params, low_valid, bg):
    pw = p_a.shape[1]
    d_a = params[1].shape[1]
    ln = bg * (d_a // HEAD_A)
    tr = _tile(seq_len, 32, V7X_SUBLANES)
    assert row0 % tr == 0 and batch % bg == 0
    off, tpb, hb = row0 // tr, seq_len // tr, tr // V7X_SUBLANES
    xs = [pl.BlockSpec((tr, pw), lambda g, i, b=b: (off + (g * bg + b) * tpb + i, 0)) for b in range(bg)]
    halos = [pl.BlockSpec((V7X_SUBLANES, pw),
                          lambda g, i, b=b: (jnp.maximum((off + (g * bg + b) * tpb + i) * hb - 1, 0), 0))
             for b in range(bg)]
    full = lambda a: pl.BlockSpec(a.shape, lambda g, i: (0, 0))
    lanes = pl.BlockSpec((1, HEAD_A, tr, ln), lambda g, i: (g, 0, i, 0))
    rowsp = pl.BlockSpec((bg, tr, d_a), lambda g, i: (g, i, 0))
    lanes_shape = jax.ShapeDtypeStruct((batch // bg, HEAD_A, seq_len, ln), F32)
    rows_shape = jax.ShapeDtypeStruct((batch, seq_len, d_a), F32)
    tail = pl.BlockSpec((bg, V7X_SUBLANES, pw), lambda g, i: (g, 0, 0))
    r, w, k, a, v, g, tails = pl.pallas_call(
        functools.partial(_rwkv_pre_lanes_kernel, bg=bg, d_a=d_a, low_valid=low_valid, tr=tr,
                          seq_len=seq_len),
        grid=(batch // bg, tpb),
        in_specs=xs + halos + [full(p) for p in params],
        out_specs=[lanes] * 4 + [rowsp] * 2 + [tail],
        out_shape=[lanes_shape] * 4 + [rows_shape] * 2
        + [jax.ShapeDtypeStruct((batch, V7X_SUBLANES, pw), F32)],
        compiler_params=_cp(("arbitrary", "arbitrary")),
        name="rwkv_pre_lanes",
    )(*([p_a] * (2 * bg)), *params)
    return (r, w, k, a, v.reshape(batch * seq_len, d_a), g.reshape(batch * seq_len, d_a),
            tails[:, V7X_SUBLANES - 1])


def _scan_chunk_setup(seq_refs, par_refs, scr_refs):
    r_ref, _, k_ref, a_ref, _ = seq_refs
    kk_ref, ka_ref, rk_ref = par_refs[:3]
    av_ref, bv_ref, km_ref, bonus_ref = scr_refs
    row = lambda ref, j: ref[pl.ds(j, 1), :]
    hd = k_ref.shape[1]
    ss = jnp.zeros(bonus_ref.shape, F32)
    for j in range(hd):
        kk = k_ref[0, j] * row(kk_ref, j)
        ss = ss + kk * kk
    inv = 1.0 / jnp.maximum(jnp.sqrt(ss), 1e-12)
    bonus = jnp.zeros(bonus_ref.shape, F32)
    for j in range(hd):
        k, a = k_ref[0, j], a_ref[0, j]
        kk = (k * row(kk_ref, j)) * inv
        kmod = k * (1.0 + (a - 1.0) * row(ka_ref, j))
        av_ref[j] = -kk
        bv_ref[j] = kk * a
        km_ref[j] = kmod
        bonus = bonus + (r_ref[0, j] * kmod) * row(rk_ref, j)
    bonus_ref[...] = bonus


def _scan_step(t, seq_refs, par_refs, scr_refs, y_ref, s_ref, j_loop):
    r_ref, w_ref, _, _, v_ref = seq_refs
    lnw_ref, lnb_ref = par_refs[3:]
    av_ref, bv_ref, km_ref, bonus_ref = scr_refs
    at = pl.ds(t, 1)
    v = v_ref[0, t]
    sa = j_loop(lambda j, acc: acc + s_ref[0, j] * av_ref[j, at, :], jnp.zeros_like(v))

    def update(j, y):
        sj = (s_ref[0, j] * w_ref[0, j, at, :] + sa * bv_ref[j, at, :]) + v * km_ref[j, at, :]
        s_ref[0, j] = sj
        return y + sj * r_ref[0, j, at, :]

    y = j_loop(update, jnp.zeros_like(v))
    mu = jnp.mean(y, axis=0, keepdims=True)
    yc = y - mu
    var = jnp.mean(yc * yc, axis=0, keepdims=True)
    yn = (yc * lax.rsqrt(var + GN_EPS)) * lnw_ref[...] + lnb_ref[...]
    y_ref[0, t] = yn + bonus_ref[at, :] * v


def _blocked_j_loop(hd):
    def run(body, init):
        def block(jb, carry):
            for jj in range(SCAN_J_BLOCK):
                carry = body(jb * SCAN_J_BLOCK + jj, carry)
            return carry
        return lax.fori_loop(0, hd // SCAN_J_BLOCK, block, init)
    return run


def _unrolled_j_loop(hd):
    def run(body, init):
        carry = init
        for j in range(hd):
            carry = body(j, carry)
        return carry
    return run


def _scan_kernel(*refs, tc):
    seq_refs, par_refs, s0_ref = refs[:5], refs[5:10], refs[10]
    y_ref, s_ref = refs[11:13]
    scr_refs = refs[13:]

    @pl.when(pl.program_id(1) == 0)
    def _():
        s_ref[...] = s0_ref[...]

    _scan_chunk_setup(seq_refs, par_refs, scr_refs)
    j_loop = _blocked_j_loop(s_ref.shape[1])

    def step(t, carry):
        _scan_step(t, seq_refs, par_refs, scr_refs, y_ref, s_ref, j_loop)
        return carry

    lax.fori_loop(0, tc, step, 0)


def _scan_project_kernel(*refs, tcf, n_tiles, n_col_tiles, n_plain):
    x_ref, wa_ref, wb_ref = refs[:3]
    seq_refs, par_refs = refs[3:8], refs[8:13]
    plain_ref, gate_ref, y_ref, s_ref = refs[13:17]
    w_scr, acc_ref, scr_refs = refs[17], refs[18], refs[19:]
    step_id = pl.program_id(0)

    @pl.when(step_id == 0)
    def _():
        s_ref[...] = jnp.zeros_like(s_ref)

    head = wa_ref.shape[0]
    w_scr[:head] = wa_ref[...].astype(BF16)
    w_scr[head:] = wb_ref[:w_scr.shape[0] - head].astype(BF16)

    _scan_chunk_setup(seq_refs, par_refs, scr_refs)
    j_loop = _unrolled_j_loop(s_ref.shape[1])
    slab = x_ref.shape[0] // PROJECT_SLABS
    for t in range(tcf):
        if t % (tcf // PROJECT_SLABS) == 0:
            rows = pl.ds((t // (tcf // PROJECT_SLABS)) * slab, slab)
            acc_ref[rows, :] = lax.dot_general(x_ref[rows, :], w_scr[...], _NT_DIMS,
                                               preferred_element_type=F32)
        _scan_step(t, seq_refs, par_refs, scr_refs, y_ref, s_ref, j_loop)

    col_tile = jnp.minimum(step_id, n_tiles - 1) % n_col_tiles

    @pl.when(col_tile < n_plain)
    def _():
        plain_ref[...] = acc_ref[...]

    @pl.when(col_tile >= n_plain)
    def _():
        gate_ref[...] = jax.nn.sigmoid(acc_ref[...]).astype(gate_ref.dtype)


def _scan(r, w, k, a, v, params, s0):
    g, hd, t, ln = r.shape
    tc = _tile(t, 64, 1)
    by_chan = pl.BlockSpec((1, hd, tc, ln), lambda gi, c: (gi, 0, c, 0))
    by_time = pl.BlockSpec((1, tc, hd, ln), lambda gi, c: (gi, c, 0, 0))
    per_group = params[0].shape[0] > 1
    par = pl.BlockSpec((None, hd, ln), lambda gi, c: (gi if per_group else 0, 0, 0))
    st = pl.BlockSpec((1, hd, hd, ln), lambda gi, c: (gi, 0, 0, 0))
    return pl.pallas_call(
        functools.partial(_scan_kernel, tc=tc),
        grid=(g, t // tc),
        in_specs=[by_chan] * 4 + [by_time] + [par] * 5 + [st],
        out_specs=[by_time, st],
        out_shape=[jax.ShapeDtypeStruct(v.shape, F32), jax.ShapeDtypeStruct(s0.shape, F32)],
        scratch_shapes=[pltpu.VMEM((hd, tc, ln), F32)] * 3 + [pltpu.VMEM((tc, ln), F32)],
        compiler_params=_cp(("arbitrary", "arbitrary")),
        name="rwkv_scan",
    )(r, w, k, a, v, *params, s0)


def _scan_project_fits(m, t, groups, n_rows, tn=512):
    tiles = (m // _tile(m, 1100)) * (n_rows // tn)
    return groups == 1 and n_rows % tn == 0 and t % SCAN_FUSED_STEPS == 0 and t // SCAN_FUSED_STEPS >= tiles


def _scan_project(x, wt, row0, plain_rows, gate_rows, r, w, k, a, v, params, tn=512):
    m, kdim = x.shape
    _, hd, t, ln = r.shape
    tm = _tile(m, 1100)
    n_plain, n_gate = plain_rows // tn, gate_rows // tn
    nct = n_plain + n_gate
    tiles = (m // tm) * nct
    tcf = SCAN_FUSED_STEPS
    assert plain_rows % tn == 0 and gate_rows % tn == 0 and t // tcf >= tiles
    head = tn - row0 % tn
    assert head % 16 == 0 and row0 % head == 0 and tn % head == 0
    ja, jb, stride = row0 // head, (row0 + head) // tn, tn // head
    tile = lambda s: jnp.minimum(s, tiles - 1)
    row_t = lambda s: tile(s) // nct
    col_t = lambda s: tile(s) % nct
    by_chan = pl.BlockSpec((1, hd, tcf, ln), lambda s: (0, 0, s, 0))
    by_time = pl.BlockSpec((1, tcf, hd, ln), lambda s: (0, s, 0, 0))
    par = pl.BlockSpec((None, hd, ln), lambda s: (0, 0, 0))
    st = pl.BlockSpec((1, hd, hd, ln), lambda s: (0, 0, 0, 0))
    return pl.pallas_call(
        functools.partial(_scan_project_kernel, tcf=tcf, n_tiles=tiles, n_col_tiles=nct, n_plain=n_plain),
        grid=(t // tcf,),
        in_specs=[pl.BlockSpec((tm, kdim), lambda s: (row_t(s), 0), pipeline_mode=pl.Buffered(1)),
                  pl.BlockSpec((head, kdim), lambda s: (ja + stride * col_t(s), 0)),
                  pl.BlockSpec((tn, kdim), lambda s: (jb + col_t(s), 0))]
        + [by_chan] * 4 + [by_time] + [par] * 5,
        out_specs=[pl.BlockSpec((tm, tn), lambda s: (row_t(s), jnp.minimum(col_t(s), n_plain - 1))),
                   pl.BlockSpec((tm, tn), lambda s: (row_t(s), jnp.maximum(col_t(s) - n_plain, 0))),
                   by_time, st],
        out_shape=[jax.ShapeDtypeStruct((m, plain_rows), F32), jax.ShapeDtypeStruct((m, gate_rows), BF16),
                   jax.ShapeDtypeStruct(v.shape, F32), jax.ShapeDtypeStruct((1, hd, hd, ln), F32)],
        scratch_shapes=[pltpu.VMEM((tn, kdim), BF16), pltpu.VMEM((tm, tn), F32)]
        + [pltpu.VMEM((hd, tcf, ln), F32)] * 3 + [pltpu.VMEM((tcf, ln), F32)],
        compiler_params=_cp(("arbitrary",)),
        name="rwkv_scan_project",
    )(x, wt, wt, r, w, k, a, v, *params)


class _Lanes:
    def __init__(self, b, n_heads):
        self.h = n_heads
        self.by_batch = b % V7X_LANES == 0
        self.bi = V7X_LANES if self.by_batch else max(1, min(b, V7X_LANES // n_heads))
        self.bo = b // self.bi
        self.groups = self.bo * n_heads if self.by_batch else self.bo
        self.lanes = self.bi if self.by_batch else self.bi * n_heads

    def rows_by_time(self, x2d, t):
        x = x2d.reshape(self.bo, self.bi, t, self.h, HEAD_A)
        x = x.transpose(0, 3, 2, 4, 1) if self.by_batch else x.transpose(0, 2, 4, 1, 3)
        return x.reshape(self.groups, t, HEAD_A, self.lanes)

    def rows_by_chan(self, x2d, t):
        x = x2d.reshape(self.bo, self.bi, t, HEAD_A, self.h)
        x = x.transpose(0, 4, 3, 2, 1) if self.by_batch else x.transpose(0, 3, 2, 1, 4)
        return x.reshape(self.groups, HEAD_A, t, self.lanes)

    def rows_back(self, y, t):
        if self.by_batch:
            y = y.reshape(self.bo, self.h, t, HEAD_A, self.bi).transpose(0, 4, 2, 1, 3)
        else:
            y = y.reshape(self.bo, t, HEAD_A, self.bi, self.h).transpose(0, 3, 1, 4, 2)
        return y.reshape(self.bo * self.bi * t, self.h * HEAD_A)

    def param(self, p):
        p = p.reshape(self.h, HEAD_A)
        if self.by_batch:
            p = jnp.broadcast_to(p[None, :, :, None], (self.bo, self.h, HEAD_A, self.bi))
            return p.reshape(self.groups, HEAD_A, self.lanes)
        return jnp.tile(p.T, (1, self.bi))[None]

    def state_in(self, s):
        s = s.astype(F32).reshape(self.bo, self.bi, self.h, HEAD_A, HEAD_A)
        s = s.transpose(0, 2, 4, 3, 1) if self.by_batch else s.transpose(0, 4, 3, 1, 2)
        return s.reshape(self.groups, HEAD_A, HEAD_A, self.lanes)

    def state_out(self, s):
        if self.by_batch:
            s = s.reshape(self.bo, self.h, HEAD_A, HEAD_A, self.bi).transpose(0, 4, 1, 3, 2)
        else:
            s = s.reshape(self.bo, HEAD_A, HEAD_A, self.bi, self.h).transpose(0, 3, 4, 2, 1)
        return s.reshape(self.bo * self.bi, self.h, HEAD_A, HEAD_A)


def _rwkv_group(lay, rwka, v, t, s_prev, scan_params):
    params = [lay.param(p) for p in scan_params]
    if s_prev is None:
        s0 = jnp.zeros((lay.groups, HEAD_A, HEAD_A, lay.lanes), F32)
    else:
        s0 = lay.state_in(s_prev)
    y, s_new = _scan(*rwka, lay.rows_by_time(v, t), params, s0)
    return lay.rows_back(y, t), lay.state_out(s_new)


def _gate_kernel(*refs):
    y_ref, g_ref, o_ref = refs[-3:]
    o_ref[...] = (y_ref[...] * g_ref[...]).astype(o_ref.dtype)


def _gate_rows(y, g, row0, prev_out):
    rows, d_a = y.shape
    tr = _tile(_gcd(row0, rows) if row0 else rows, 512, 16)
    off = row0 // tr
    row = pl.BlockSpec((tr, d_a), lambda i: (i, 0))
    return pl.pallas_call(
        _gate_kernel,
        grid=(rows // tr,),
        in_specs=[pl.BlockSpec(memory_space=pl.ANY), row, row],
        out_specs=pl.BlockSpec((tr, d_a), lambda i: (off + i, 0)),
        out_shape=jax.ShapeDtypeStruct(prev_out.shape, BF16),
        input_output_aliases={0: 0},
        compiler_params=_cp(("arbitrary",)),
        name="rwkv_gate",
    )(prev_out, y, g)


def _conv_kernel(*refs, tr, seq_len, mode):
    gb_ref, gc_ref, u_ref, aux1_ref, aux2_ref, cw_ref = refs[-8:-2]
    y_ref, u2_ref = refs[-2:]
    f1_ref, f2_ref = aux1_ref, aux2_ref
    u2 = gc_ref[...] * u_ref[...]
    if mode == "halo":
        halo = aux1_ref[...] * aux2_ref[...]
        row0 = pl.program_id(0) * tr
        p1 = _shift_rows(u2, 1, row0, seq_len, halo)
        p2 = _shift_rows(u2, 2, row0, seq_len, halo)
    else:
        t = lax.broadcasted_iota(jnp.int32, u2.shape, 0) % seq_len
        p1 = jnp.where(t < 1, f1_ref[...], pltpu.roll(u2, 1, axis=0))
        p2 = jnp.where(t < 2, f2_ref[...], pltpu.roll(u2, 2, axis=0))
    z = p2 * cw_ref[0:1, :] + p1 * cw_ref[1:2, :]
    z = z + u2 * cw_ref[2:3, :]
    y_ref[...] = (gb_ref[...] * z).astype(y_ref.dtype)
    u2_ref[...] = u2


def _conv(p_bc, row0, rows, seq_len, firsts, conv_w, prev_out):
    n = p_bc.shape[0]
    d_b = conv_w.shape[1]
    mode = "halo" if firsts is None else "first"
    if mode == "halo":
        tr = _tile(_gcd(seq_len, rows), 256, 16)
    else:
        tr = _tile(rows, 256, _lcm(seq_len, 16))
    assert row0 % tr == 0 and rows % tr == 0 and seq_len >= CONV_W - 1
    off = row0 // tr
    col = lambda c: pl.BlockSpec((tr, d_b), lambda i: (off + i, c))
    in_specs = [col(0), col(1), col(2)]
    args = [p_bc, p_bc, p_bc]
    if mode == "halo":
        hb = tr // V7X_SUBLANES
        hcol = lambda c: pl.BlockSpec(
            (V7X_SUBLANES, d_b), lambda i: (jnp.maximum((off + i) * hb - 1, 0), c))
        in_specs += [hcol(1), hcol(2)]
        args += [p_bc, p_bc]
    else:
        in_specs += [pl.BlockSpec((tr, d_b), lambda i: (i, 0))] * 2
        args += list(firsts)
    in_specs.append(pl.BlockSpec(conv_w.shape, lambda i: (0, 0)))
    args.append(conv_w)
    in_specs = [pl.BlockSpec(memory_space=pl.ANY)] * 2 + in_specs
    args = list(prev_out) + args
    out_spec = pl.BlockSpec((tr, d_b), lambda i: (off + i, 0))
    return pl.pallas_call(
        functools.partial(_conv_kernel, tr=tr, seq_len=seq_len, mode=mode),
        grid=(rows // tr,),
        in_specs=in_specs,
        out_specs=[out_spec, out_spec],
        out_shape=[jax.ShapeDtypeStruct((n, d_b), BF16), jax.ShapeDtypeStruct((n, d_b), F32)],
        input_output_aliases={0: 0, 1: 1},
        compiler_params=_cp(("arbitrary",)),
        name="conv_" + mode,
    )(*args)


def _attn_kernel(*refs, bb, tq, heads):
    q_ref, k_ref, v_ref, o_ref = refs[-4:]
    dh = q_ref.shape[1] // heads
    scale = dh ** -0.5
    rows = bb * tq
    row_batch = lax.broadcasted_iota(jnp.int32, (rows, dh), 0) // tq
    cols = [slice(h * dh, (h + 1) * dh) for h in range(heads)]
    qs = [q_ref[:, c].astype(BF16) for c in cols]
    pairs = [(b, h) for b in range(bb) for h in range(heads)]
    s = jnp.concatenate(
        [lax.dot_general(qs[h], k_ref[b, :, cols[h]].astype(BF16), _NT_DIMS, preferred_element_type=F32)
         for b, h in pairs], axis=0) * scale
    p = jnp.exp(s - jnp.max(s, axis=-1, keepdims=True))
    p = (p / jnp.sum(p, axis=-1, keepdims=True)).astype(BF16)
    outs = [None] * heads
    for n, (b, h) in enumerate(pairs):
        o = jnp.dot(p[n * rows:(n + 1) * rows], v_ref[b, :, cols[h]].astype(BF16), preferred_element_type=F32)
        outs[h] = o if outs[h] is None else jnp.where(row_batch == b, o, outs[h])
    for h in range(heads):
        o_ref[:, cols[h]] = outs[h].astype(o_ref.dtype)


def _attention(q_src, q_col, row0, b, t, mem_k, mem_v, heads, prev_out):
    n_mem, d_c = mem_k.shape[1:]
    if t >= 128:
        bb, tq = 1, _tile(t, 512, 16)
    else:
        kv_block_cap = max(1, ATTN_KV_BLOCK_BYTES // (n_mem * d_c * mem_k.dtype.itemsize))
        bb, tq = _tile(b, min(kv_block_cap, max(1, 64 // t)), 1), t
    assert (bb * tq) % 16 == 0 and row0 % (bb * tq) == 0
    tpb = t // tq
    off = row0 // (bb * tq)
    qmap = lambda i, j: (off + i * tpb + j, q_col)
    omap = lambda i, j: (off + i * tpb + j, 0)
    kv = pl.BlockSpec((bb, n_mem, d_c), lambda i, j: (i, 0, 0))
    return pl.pallas_call(
        functools.partial(_attn_kernel, bb=bb, tq=tq, heads=heads),
        grid=(b // bb, tpb),
        in_specs=[pl.BlockSpec(memory_space=pl.ANY), pl.BlockSpec((bb * tq, d_c), qmap), kv, kv],
        out_specs=pl.BlockSpec((bb * tq, d_c), omap),
        out_shape=jax.ShapeDtypeStruct(prev_out.shape, BF16),
        input_output_aliases={0: 0},
        compiler_params=_cp(("arbitrary", "arbitrary")),
        name="mem_attention",
    )(prev_out, q_src, mem_k, mem_v)


def _merge_kernel(ya_ref, yb_ref, yc_ref, ga_ref, gb_ref, gc_ref, wa_ref, wb_ref, wc_ref, o_ref):
    def branch(y_ref, g_ref, w_ref):
        return g_ref[...].astype(F32) * jnp.dot(
            y_ref[...], w_ref[...].astype(BF16), preferred_element_type=F32)

    acc = branch(ya_ref, ga_ref, wa_ref) + branch(yb_ref, gb_ref, wb_ref)
    o_ref[...] = (acc + branch(yc_ref, gc_ref, wc_ref)).astype(o_ref.dtype)


def _merge(ya, yb, yc, gates, wa, wb, wc):
    n, d = ya.shape[0], wa.shape[1]
    tm = _tile(n, 1100)
    tn = _tile(d, 512, V7X_LANES)
    nj = d // tn
    act = lambda y: pl.BlockSpec((tm, y.shape[1]), lambda i, j: (i, 0))
    gate = lambda c: pl.BlockSpec((tm, tn), lambda i, j: (i, c * nj + j))
    wsp = lambda w: pl.BlockSpec((w.shape[0], tn), lambda i, j: (0, j))
    return pl.pallas_call(
        _merge_kernel,
        grid=(n // tm, nj),
        in_specs=[act(ya), act(yb), act(yc), gate(0), gate(1), gate(2), wsp(wa), wsp(wb), wsp(wc)],
        out_specs=pl.BlockSpec((tm, tn), lambda i, j: (i, j)),
        out_shape=jax.ShapeDtypeStruct((n, d), BF16),
        compiler_params=_cp(("arbitrary", "arbitrary")),
        name="branch_merge",
    )(ya, yb, yc, gates, gates, gates, wa, wb, wc)


def _swap_head_chan(p, n_heads, to_chan_major, axis=-1):
    axis = axis % p.ndim
    blocks = p.shape[axis] // (n_heads * HEAD_A)
    inner = (n_heads, HEAD_A) if to_chan_major else (HEAD_A, n_heads)
    q = p.reshape(p.shape[:axis] + (blocks,) + inner + p.shape[axis + 1:])
    return jnp.swapaxes(q, axis + 1, axis + 2).reshape(p.shape)


def _pad_rows(w, row0, total):
    return jnp.zeros((total, w.shape[1]), BF16).at[row0:row0 + w.shape[0]].set(w.astype(BF16))


def _first_rows(state, seq_len, width):
    b, k, c = state.shape
    out = jnp.zeros((b, seq_len, width), F32).at[:, :k, :c].set(state.astype(F32))
    return out.reshape(b * seq_len, width)


def _layer(x, h1, groups, lp, mem_kv, heads_c):
    n, d = x.shape
    d_a = lp["w0"].shape[0]
    d_b = lp["conv_w"].shape[1]
    p_a_w = lp["mu_shift"].shape[0]
    lora_w, lora_a, lora_g = (lp[q].shape[0] for q in ("w_lora_up", "a_lora_up", "g_lora_up"))
    d_c = lp["w_br_c"].shape[0]
    assert d_c == d_b and p_a_w == 3 * d_a + lora_w + lora_a + lora_g

    x = _matmul(_gateup(h1, lp["w_ffn1_in"]), lp["w_ffn1_out"], residual=x, tn=256, x_resident=True)
    h = _norm(x, lp["g_mix"])

    w_t = jnp.swapaxes(lp["w_in"], 0, 1)
    pw = _round_up(p_a_w, 512)
    w_a = jnp.concatenate([_swap_head_chan(w_t[:2 * d_a], d_a // HEAD_A, to_chan_major=True, axis=0),
                           w_t[2 * d_a:pw]], axis=0).astype(BF16)
    p_a = _matmul_nt(h, w_a)
    p_bc = gates = None

    pw = p_a.shape[1]
    low_w = pw - 3 * d_a
    n_heads = d_a // HEAD_A
    rk_swap = lambda p: jnp.concatenate(
        [_swap_head_chan(p[..., :2 * d_a], n_heads, to_chan_major=True), p[..., 2 * d_a:]], axis=-1)
    to_cm = lambda p: _swap_head_chan(p, n_heads, to_chan_major=True)
    pre_params = (
        jnp.pad(rk_swap(lp["mu_shift"]), (0, pw - p_a_w)).reshape(1, pw),
        to_cm(lp["w0"]).reshape(1, d_a), to_cm(lp["a0"]).reshape(1, d_a),
        _pad_rows(to_cm(lp["w_lora_up"]), 0, low_w),
        _pad_rows(to_cm(lp["a_lora_up"]), lora_w, low_w),
        _pad_rows(lp["g_lora_up"], lora_w + lora_a, low_w),
    )
    scan_params = (lp["k_k"], lp["k_a"], lp["r_k"].reshape(-1), lp["lnx_w"], lp["lnx_b"])
    ya, new_states = jnp.zeros((n, d_a), BF16), []
    for row0, b, t, st in groups:
        lay = _Lanes(b, n_heads)
        if st is None:
            assert not lay.by_batch
            *rwka, v, g, last = _rwkv_pre_lanes(p_a, row0, b, t, pre_params, p_a_w - 3 * d_a, lay.bi)
            last = last[:, :p_a_w]
        else:
            first = _first_rows(rk_swap(st["shift"])[:, None, :], t, pw)
            r, w, k, v, a, g = _rwkv_pre_first(p_a, row0, b * t, t, first, pre_params, p_a_w - 3 * d_a)
            rwka = [lay.rows_by_chan(q, t) for q in (r, w, k, a)]
            last = p_a[row0 + t - 1:row0 + b * t:t, :p_a_w]
        rest_rows = (3 * d_b + d_c, 3 * d)
        if p_bc is None and st is None and _scan_project_fits(n, t, lay.groups, sum(rest_rows)):
            p_bc, gates, y, s_new = _scan_project(
                h, w_t, p_a_w, *rest_rows, *rwka, lay.rows_by_time(v, t),
                [lay.param(p) for p in scan_params])
            y, s_new = lay.rows_back(y, t), lay.state_out(s_new)
        else:
            y, s_new = _rwkv_group(lay, rwka, v, t, None if st is None else st["rwkv"], scan_params)
        ya = _gate_rows(y, g, row0, ya)
        shift_new = jnp.concatenate(
            [_swap_head_chan(last[:, :2 * d_a], n_heads, to_chan_major=False), last[:, 2 * d_a:]], axis=1)
        new_states.append({"rwkv": s_new, "shift": shift_new})

    if p_bc is None:
        p_bc = _matmul_nt(h, w_t, row0=p_a_w, rows=3 * d_b + d_c)
        gates = _matmul_nt(h, w_t, row0=p_a_w + 3 * d_b + d_c, rows=3 * d, out_dtype=BF16, act="sigmoid")

    conv = (jnp.zeros((n, d_b), BF16), jnp.zeros((n, d_b), F32))
    for row0, b, t, st in groups:
        firsts = None
        if st is not None:
            buf = st["conv"]
            firsts = (_first_rows(buf[:, 1:, :], t, d_b), _first_rows(buf, t, d_b))
        conv = _conv(p_bc, row0, b * t, t, firsts, lp["conv_w"], conv)
    yb, u2 = conv
    for (row0, b, t, st), ns in zip(groups, new_states):
        tail = [u2[row0 + t - q:row0 + b * t:t] for q in range(CONV_W - 1, 0, -1)]
        ns["conv"] = jnp.stack(tail, axis=1)

    yc = jnp.zeros((n, d_c), BF16)
    for (row0, b, t, st), (mk, mv) in zip(groups, mem_kv):
        yc = _attention(p_bc, (3 * d_b) // d_c, row0, b, t, mk.reshape(b, -1, d_c).astype(BF16),
                        mv.reshape(b, -1, d_c).astype(BF16), heads_c, yc)

    merged = _merge(ya, yb, yc, gates, lp["w_br_a"], lp["w_br_b"], lp["w_br_c"])
    x = _matmul(merged, lp["w_out"], residual=x)

    h = _norm(x, lp["g_ffn2"])
    x = _matmul(_gateup(h, lp["w_ffn2_in"]), lp["w_ffn2_out"], residual=x, tn=256, x_resident=True)
    return x, new_states


def kernel(x_prompt, x_sample, mem_prompt, cache_mem_k, cache_mem_v, state_rwkv, state_shift, state_conv, g_ffn1, w_ffn1_in, w_ffn1_out, g_mix, w_in, mu_shift, w0, w_lora_up, a0, a_lora_up, g_lora_up, k_k, k_a, r_k, lnx_w, lnx_b, conv_w, g_mem, w_mem_kv, w_br_a, w_br_b, w_br_c, w_out, g_ffn2, w_ffn2_in, w_ffn2_out, g_final):
    bp, tp, d = x_prompt.shape
    bs, ts, _ = x_sample.shape
    n_p, n_s = bp * tp, bs * ts
    depth = w_in.shape[0]
    n_mem, heads_c, dh_c = cache_mem_k.shape[2:]
    d_c = heads_c * dh_c

    per_layer = dict(
        g_ffn1=g_ffn1, w_ffn1_in=w_ffn1_in, w_ffn1_out=w_ffn1_out, g_mix=g_mix, w_in=w_in,
        mu_shift=mu_shift, w0=w0, w_lora_up=w_lora_up, a0=a0, a_lora_up=a_lora_up,
        g_lora_up=g_lora_up, k_k=k_k, k_a=k_a, r_k=r_k, lnx_w=lnx_w, lnx_b=lnx_b, conv_w=conv_w,
        w_br_a=w_br_a, w_br_b=w_br_b, w_br_c=w_br_c, w_out=w_out, g_ffn2=g_ffn2,
        w_ffn2_in=w_ffn2_in, w_ffn2_out=w_ffn2_out)

    mem_rows = mem_prompt.reshape(bp * n_mem, d)
    outs = {q: [] for q in ("mk", "mv", "rp", "shp", "cp", "rs", "shs", "cs")}
    x = None
    for l in range(depth):
        lp = {q: p[l] for q, p in per_layer.items()}
        if l == 0:
            x, h1 = _norm_concat(x_prompt.reshape(n_p, d), x_sample.reshape(n_s, d), lp["g_ffn1"])
        else:
            h1 = _norm(x, lp["g_ffn1"])
        kv = _matmul(_norm(mem_rows, g_mem[l]), w_mem_kv[l])
        mk_p = kv[:, :d_c].reshape(bp, n_mem, heads_c, dh_c)
        mv_p = kv[:, d_c:2 * d_c].reshape(bp, n_mem, heads_c, dh_c)
        groups = [
            (0, bp, tp, None),
            (n_p, bs, ts, {"rwkv": state_rwkv[l], "shift": state_shift[l], "conv": state_conv[l]}),
        ]
        prompt_kv = (kv[:, :d_c].reshape(bp, n_mem, d_c), kv[:, d_c:2 * d_c].reshape(bp, n_mem, d_c))
        x, (st_p, st_s) = _layer(
            x, h1, groups, lp, [prompt_kv, (cache_mem_k[l], cache_mem_v[l])], heads_c)
        outs["mk"].append(mk_p)
        outs["mv"].append(mv_p)
        for tag, st in (("p", st_p), ("s", st_s)):
            outs["r" + tag].append(st["rwkv"])
            outs["sh" + tag].append(st["shift"])
            outs["c" + tag].append(st["conv"])
    y_prompt = _norm(x, g_final, F32, 0, n_p).reshape(bp, tp, d)
    y_sample = _norm(x, g_final, F32, n_p, n_s).reshape(bs, ts, d)
    stack = lambda q: jnp.stack(outs[q], 0)
    return (y_prompt, y_sample, stack("mk"), stack("mv"), stack("rp"), stack("shp"), stack("cp"),
            stack("rs"), stack("shs"), stack("cs"))
```

```python
import functools

import jax
import jax.numpy as jnp
from jax import lax
from jax.experimental import pallas as pl
from jax.experimental.pallas import tpu as pltpu

F32 = jnp.float32
BF16 = jnp.bfloat16

EPS = 1e-6
GN_EPS = 64e-5
HEAD_A = 64
CONV_W = 3
V7X_LANES = 128
V7X_SUBLANES = 8
V7X_VMEM_LIMIT = 56 * 1024 * 1024
W_LOG_SHIFT = 0.5
ATTN_KV_BLOCK_BYTES = 4 * 1024 * 1024
GATEUP_ROW_TILE_CAP = 2200
SCAN_J_BLOCK = 8
SCAN_FUSED_STEPS = 8
PROJECT_SLABS = 4


def _cp(sem, vmem=V7X_VMEM_LIMIT):
    return pltpu.CompilerParams(dimension_semantics=sem, vmem_limit_bytes=vmem)


def _tile(n, cap, mult=16):
    best = None
    for t in range(mult, min(n, cap) + 1, mult):
        if n % t == 0:
            best = t
    return best if best is not None else n


def _round_up(n, m):
    return (n + m - 1) // m * m


def _rms(x, g):
    ms = jnp.mean(x * x, axis=-1, keepdims=True)
    return (x * lax.rsqrt(ms + EPS)) * g


def _norm2_kernel(xp_ref, xs_ref, g_ref, x_ref, h_ref, *, n_prompt_tiles):
    i = pl.program_id(0)

    def emit(x):
        x_ref[...] = x
        h_ref[...] = _rms(x, g_ref[...]).astype(BF16)

    @pl.when(i < n_prompt_tiles)
    def _():
        emit(xp_ref[...])

    @pl.when(i >= n_prompt_tiles)
    def _():
        emit(xs_ref[...])


def _norm_concat(xp, xs, g):
    (n_p, d), n_s = xp.shape, xs.shape[0]
    tr = _tile(_gcd(n_p, n_s), 256, 8)
    npt, nst = n_p // tr, n_s // tr
    n = n_p + n_s
    return pl.pallas_call(
        functools.partial(_norm2_kernel, n_prompt_tiles=npt),
        grid=(npt + nst,),
        in_specs=[
            pl.BlockSpec((tr, d), lambda i: (jnp.minimum(i, npt - 1), 0)),
            pl.BlockSpec((tr, d), lambda i: (jnp.maximum(i - npt, 0), 0)),
            pl.BlockSpec((1, d), lambda i: (0, 0)),
        ],
        out_specs=[pl.BlockSpec((tr, d), lambda i: (i, 0)),
                   pl.BlockSpec((tr, d), lambda i: (i, 0))],
        out_shape=[jax.ShapeDtypeStruct((n, d), F32), jax.ShapeDtypeStruct((n, d), BF16)],
        compiler_params=_cp(("arbitrary",)),
        name="norm_concat",
    )(xp, xs, g.reshape(1, d))


def _gcd(a, b):
    while b:
        a, b = b, a % b
    return a


def _lcm(a, b):
    return a * b // _gcd(a, b)


def _norm_kernel(x_ref, g_ref, h_ref):
    h_ref[...] = _rms(x_ref[...], g_ref[...]).astype(h_ref.dtype)


def _norm(x, g, out_dtype=BF16, row0=0, rows=None):
    n, d = x.shape
    rows = n if rows is None else rows
    tr = _tile(_gcd(row0, rows) if row0 else rows, 256, 16)
    off = row0 // tr
    return pl.pallas_call(
        _norm_kernel,
        grid=(rows // tr,),
        in_specs=[pl.BlockSpec((tr, d), lambda i: (off + i, 0)),
                  pl.BlockSpec((1, d), lambda i: (0, 0))],
        out_specs=pl.BlockSpec((tr, d), lambda i: (i, 0)),
        out_shape=jax.ShapeDtypeStruct((rows, d), out_dtype),
        compiler_params=_cp(("arbitrary",)),
        name="norm",
    )(x, g.reshape(1, d))


def _gateup_kernel(h_ref, wg_ref, wu_ref, o_ref):
    h = h_ref[...]
    gate = jnp.dot(h, wg_ref[...].astype(BF16), preferred_element_type=F32)
    up = jnp.dot(h, wu_ref[...].astype(BF16), preferred_element_type=F32)
    o_ref[...] = ((gate * jax.nn.sigmoid(gate)) * up * 0.5).astype(o_ref.dtype)


def _gateup(h, w_in):
    n, d = h.shape
    f = w_in.shape[1] // 2
    tm = _tile(n, GATEUP_ROW_TILE_CAP)
    tn = _tile(f, 256, V7X_LANES)
    nf = f // tn
    return pl.pallas_call(
        _gateup_kernel,
        grid=(n // tm, nf),
        in_specs=[
            pl.BlockSpec((tm, d), lambda i, j: (i, 0), pipeline_mode=pl.Buffered(1)),
            pl.BlockSpec((d, tn), lambda i, j: (0, j)),
            pl.BlockSpec((d, tn), lambda i, j: (0, nf + j)),
        ],
        out_specs=pl.BlockSpec((tm, tn), lambda i, j: (i, j)),
        out_shape=jax.ShapeDtypeStruct((n, f), BF16),
        compiler_params=_cp(("arbitrary", "arbitrary")),
        name="ffn_gateup",
    )(h, w_in, w_in)


def _mm_kernel(x_ref, w_ref, o_ref, *, act):
    acc = jnp.dot(x_ref[...], w_ref[...].astype(BF16), preferred_element_type=F32)
    if act == "sigmoid":
        acc = jax.nn.sigmoid(acc)
    o_ref[...] = acc.astype(o_ref.dtype)


def _mm_res_kernel(x_ref, w_ref, r_ref, o_ref):
    o_ref[...] = r_ref[...] + jnp.dot(x_ref[...], w_ref[...].astype(BF16), preferred_element_type=F32)


_NT_DIMS = (((1,), (1,)), ((), ()))


def _mm_nt_kernel(x_ref, w_ref, o_ref, *, act):
    acc = lax.dot_general(x_ref[...], w_ref[...].astype(BF16), _NT_DIMS, preferred_element_type=F32)
    if act == "sigmoid":
        acc = jax.nn.sigmoid(acc)
    o_ref[...] = acc.astype(o_ref.dtype)


def _mm_nt_shift_kernel(x_ref, wa_ref, wb_ref, o_ref, w_scr, *, act):
    @pl.when(pl.program_id(1) == 0)
    def _():
        head = wa_ref.shape[0]
        w_scr[:head] = wa_ref[...].astype(BF16)
        w_scr[head:] = wb_ref[:w_scr.shape[0] - head].astype(BF16)

    acc = lax.dot_general(x_ref[...], w_scr[...], _NT_DIMS, preferred_element_type=F32)
    if act == "sigmoid":
        acc = jax.nn.sigmoid(acc)
    o_ref[...] = acc.astype(o_ref.dtype)


def _matmul_nt(x, wt, *, row0=0, rows=None, out_dtype=F32, act=None, tn=512):
    m, k = x.shape
    rows = wt.shape[0] - row0 if rows is None else rows
    tn = min(tn, _round_up(rows, V7X_LANES))
    nb = -(-rows // tn)
    tm = _tile(m, 1100)
    out_shape = jax.ShapeDtypeStruct((m, nb * tn), out_dtype)
    if row0 % tn == 0:
        jb = row0 // tn
        return pl.pallas_call(
            functools.partial(_mm_nt_kernel, act=act),
            grid=(m // tm, nb),
            in_specs=[pl.BlockSpec((tm, k), lambda i, j: (i, 0)),
                      pl.BlockSpec((tn, k), lambda i, j: (jb + j, 0))],
            out_specs=pl.BlockSpec((tm, tn), lambda i, j: (i, j)),
            out_shape=out_shape,
            compiler_params=_cp(("arbitrary", "arbitrary")),
            name="matmul_nt",
        )(x, wt)
    head = tn - row0 % tn
    assert head % 16 == 0 and row0 % head == 0 and tn % head == 0
    ja, jb, step = row0 // head, (row0 + head) // tn, tn // head
    return pl.pallas_call(
        functools.partial(_mm_nt_shift_kernel, act=act),
        grid=(nb, m // tm),
        in_specs=[pl.BlockSpec((tm, k), lambda j, i: (i, 0)),
                  pl.BlockSpec((head, k), lambda j, i: (ja + step * j, 0)),
                  pl.BlockSpec((tn, k), lambda j, i: (jb + j, 0))],
        out_specs=pl.BlockSpec((tm, tn), lambda j, i: (i, j)),
        out_shape=out_shape,
        scratch_shapes=[pltpu.VMEM((tn, k), BF16)],
        compiler_params=_cp(("arbitrary", "arbitrary")),
        name="matmul_nt_shifted",
    )(x, wt, wt)


def _matmul(x, w, *, col0=0, cols=None, out_dtype=F32, act=None, residual=None, tn=512,
            x_resident=False):
    m, k = x.shape
    cols = w.shape[1] - col0 if cols is None else cols
    tn = min(tn, _round_up(cols, V7X_LANES))
    assert col0 % tn == 0
    nb, jb = -(-cols // tn), col0 // tn
    tm = _tile(m, 1100)
    x_mode = pl.Buffered(1) if x_resident else None
    in_specs = [pl.BlockSpec((tm, k), lambda i, j: (i, 0), pipeline_mode=x_mode),
                pl.BlockSpec((k, tn), lambda i, j: (0, jb + j))]
    args = [x, w]
    if residual is None:
        kern = functools.partial(_mm_kernel, act=act)
    else:
        kern = _mm_res_kernel
        in_specs.append(pl.BlockSpec((tm, tn), lambda i, j: (i, j)))
        args.append(residual)
    return pl.pallas_call(
        kern,
        grid=(m // tm, nb),
        in_specs=in_specs,
        out_specs=pl.BlockSpec((tm, tn), lambda i, j: (i, j)),
        out_shape=jax.ShapeDtypeStruct((m, nb * tn), out_dtype),
        compiler_params=_cp(("arbitrary", "arbitrary")),
        name="matmul",
    )(*args)


def _shift_rows(x, k, tile_row0, seq_len, halo):
    rows = lax.broadcasted_iota(jnp.int32, x.shape, 0)
    out = pltpu.roll(x, k, axis=0)
    at_start = (tile_row0 % seq_len) == 0
    for q in range(k):
        h = halo[V7X_SUBLANES - k + q:V7X_SUBLANES - k + q + 1, :]
        h = jnp.where(at_start, jnp.zeros_like(h), h)
        out = jnp.where(rows == q, h, out)
    return out


def _lowrank_maps(low, param_refs, low_valid):
    _, w0_ref, a0_ref, lw_ref, la_ref, lg_ref = param_refs
    low = jnp.where(lax.broadcasted_iota(jnp.int32, low.shape, 1) < low_valid, low, 0.0)
    z = w0_ref[...] + jnp.dot(jnp.tanh(low).astype(BF16), lw_ref[...], preferred_element_type=F32)
    decay = jnp.exp(-jnp.exp(-W_LOG_SHIFT) * jax.nn.sigmoid(z))
    a = jax.nn.sigmoid(a0_ref[...] + jnp.dot(low.astype(BF16), la_ref[...], preferred_element_type=F32))
    g = jnp.dot(jax.nn.sigmoid(low).astype(BF16), lg_ref[...], preferred_element_type=F32)
    return decay, a, g


def _premix(x, prev, param_refs, d_a, low_valid):
    ps = x + (prev - x) * param_refs[0][...]
    decay, a, g = _lowrank_maps(ps[:, 3 * d_a:], param_refs, low_valid)
    return ps[:, :d_a], decay, ps[:, d_a:2 * d_a], ps[:, 2 * d_a:3 * d_a], a, g


def _rwkv_pre_first_kernel(x_ref, first_ref, *refs, d_a, low_valid, seq_len):
    x = x_ref[...]
    t = lax.broadcasted_iota(jnp.int32, x.shape, 0) % seq_len
    prev = jnp.where(t == 0, first_ref[...], pltpu.roll(x, 1, axis=0))
    for o_ref, val in zip(refs[6:], _premix(x, prev, refs[:6], d_a, low_valid)):
        o_ref[...] = val


def _rwkv_pre_first(p_a, row0, rows, seq_len, first, params, low_valid):
    pw = p_a.shape[1]
    d_a = params[1].shape[1]
    tr = _tile(rows, 128, _lcm(seq_len, V7X_SUBLANES))
    assert row0 % tr == 0 and rows % tr == 0
    off = row0 // tr
    full = lambda a: pl.BlockSpec(a.shape, lambda i: (0, 0))
    out_spec = pl.BlockSpec((tr, d_a), lambda i: (i, 0))
    return pl.pallas_call(
        functools.partial(_rwkv_pre_first_kernel, d_a=d_a, low_valid=low_valid, seq_len=seq_len),
        grid=(rows // tr,),
        in_specs=[pl.BlockSpec((tr, pw), lambda i: (off + i, 0)),
                  pl.BlockSpec((tr, pw), lambda i: (i, 0))] + [full(p) for p in params],
        out_specs=[out_spec] * 6,
        out_shape=[jax.ShapeDtypeStruct((rows, d_a), F32)] * 6,
        compiler_params=_cp(("arbitrary",)),
        name="rwkv_pre_first",
    )(p_a, first, *params)


def _chans_to_lanes(xs, n_heads):
    bg = len(xs)
    ln = bg * n_heads
    group = lax.broadcasted_iota(jnp.int32, (xs[0].shape[0], ln), 1) // n_heads
    out = []
    for q in range(HEAD_A // bg):
        chunk = [x[:, q * ln:(q + 1) * ln] for x in xs]
        rolled = [[c if s == 0 else pltpu.roll(c, s * n_heads, axis=1) for s in range(bg)] for c in chunk]
        for jl in range(bg):
            acc = rolled[bg - 1][(bg - 1 - jl) % bg]
            for b in range(bg - 2, -1, -1):
                acc = jnp.where(group == b, rolled[b][(b - jl) % bg], acc)
            out.append(acc)
    return out


def _rwkv_pre_lanes_kernel(*refs, bg, d_a, low_valid, tr, seq_len):
    x_refs, halo_refs = refs[:bg], refs[bg:2 * bg]
    param_refs = refs[2 * bg:2 * bg + 6]
    r_ref, w_ref, k_ref, a_ref, v_ref, g_ref, tail_ref = refs[2 * bg + 6:]
    n_heads = d_a // HEAD_A
    shifted = []
    for b in range(bg):
        x = x_refs[b][...]
        prev = _shift_rows(x, 1, pl.program_id(1) * tr, seq_len, halo_refs[b][...])
        shifted.append(x + (prev - x) * param_refs[0][...])
        tail_ref[b] = x[tr - V7X_SUBLANES:, :]
    low = jnp.concatenate([ps[:, 3 * d_a:] for ps in shifted], axis=0)
    decay, a, g = _lowrank_maps(low, param_refs, low_valid)
    mixed = []
    for b, ps in enumerate(shifted):
        rows = slice(b * tr, (b + 1) * tr)
        v_ref[b] = ps[:, 2 * d_a:3 * d_a]
        g_ref[b] = g[rows]
        mixed.append((ps[:, :d_a], decay[rows], ps[:, d_a:2 * d_a], a[rows]))
    for o_ref, vals in zip((r_ref, w_ref, k_ref, a_ref), zip(*mixed)):
        for j, val in enumerate(_chans_to_lanes(vals, n_heads)):
            o_ref[0, j] = val


def _rwkv_pre_lanes(p_a, row0, batch, seq_len, params, low_valid, bg):
    pw = p_a.shape[1]
    d_a = params[1].shape[1]
    ln = bg * (d_a // HEAD_A)
    tr = _tile(seq_len, 32, V7X_SUBLANES)
    assert row0 % tr == 0 and batch % bg == 0
    off, tpb, hb = row0 // tr, seq_len // tr, tr // V7X_SUBLANES
    xs = [pl.BlockSpec((tr, pw), lambda g, i, b=b: (off + (g * bg + b) * tpb + i, 0)) for b in range(bg)]
    halos = [pl.BlockSpec((V7X_SUBLANES, pw),
                          lambda g, i, b=b: (jnp.maximum((off + (g * bg + b) * tpb + i) * hb - 1, 0), 0))
             for b in range(bg)]
    full = lambda a: pl.BlockSpec(a.shape, lambda g, i: (0, 0))
    lanes = pl.BlockSpec((1, HEAD_A, tr, ln), lambda g, i: (g, 0, i, 0))
    rowsp = pl.BlockSpec((bg, tr, d_a), lambda g, i: (g, i, 0))
    lanes_shape = jax.ShapeDtypeStruct((batch // bg, HEAD_A, seq_len, ln), F32)
    rows_shape = jax.ShapeDtypeStruct((batch, seq_len, d_a), F32)
    tail = pl.BlockSpec((bg, V7X_SUBLANES, pw), lambda g, i: (g, 0, 0))
    r, w, k, a, v, g, tails = pl.pallas_call(
        functools.partial(_rwkv_pre_lanes_kernel, bg=bg, d_a=d_a, low_valid=low_valid, tr=tr,
                          seq_len=seq_len),
        grid=(batch // bg, tpb),
        in_specs=xs + halos + [full(p) for p in params],
        out_specs=[lanes] * 4 + [rowsp] * 2 + [tail],
        out_shape=[lanes_shape] * 4 + [rows_shape] * 2
        + [jax.ShapeDtypeStruct((batch, V7X_SUBLANES, pw), F32)],
        compiler_params=_cp(("arbitrary", "arbitrary")),
        name="rwkv_pre_lanes",
    )(*([p_a] * (2 * bg)), *params)
    return (r, w, k, a, v.reshape(batch * seq_len, d_a), g.reshape(batch * seq_len, d_a),
            tails[:, V7X_SUBLANES - 1])


def _scan_chunk_setup(seq_refs, par_refs, scr_refs):
    r_ref, _, k_ref, a_ref, _ = seq_refs
    kk_ref, ka_ref, rk_ref = par_refs[:3]
    av_ref, bv_ref, km_ref, bonus_ref = scr_refs
    row = lambda ref, j: ref[pl.ds(j, 1), :]
    hd = k_ref.shape[1]
    ss = jnp.zeros(bonus_ref.shape, F32)
    for j in range(hd):
        kk = k_ref[0, j] * row(kk_ref, j)
        ss = ss + kk * kk
    inv = 1.0 / jnp.maximum(jnp.sqrt(ss), 1e-12)
    bonus = jnp.zeros(bonus_ref.shape, F32)
    for j in range(hd):
        k, a = k_ref[0, j], a_ref[0, j]
        kk = (k * row(kk_ref, j)) * inv
        kmod = k * (1.0 + (a - 1.0) * row(ka_ref, j))
        av_ref[j] = -kk
        bv_ref[j] = kk * a
        km_ref[j] = kmod
        bonus = bonus + (r_ref[0, j] * kmod) * row(rk_ref, j)
    bonus_ref[...] = bonus


def _scan_step(t, seq_refs, par_refs, scr_refs, y_ref, s_ref, j_loop):
    r_ref, w_ref, _, _, v_ref = seq_refs
    lnw_ref, lnb_ref = par_refs[3:]
    av_ref, bv_ref, km_ref, bonus_ref = scr_refs
    at = pl.ds(t, 1)
    v = v_ref[0, t]
    sa = j_loop(lambda j, acc: acc + s_ref[0, j] * av_ref[j, at, :], jnp.zeros_like(v))

    def update(j, y):
        sj = (s_ref[0, j] * w_ref[0, j, at, :] + sa * bv_ref[j, at, :]) + v * km_ref[j, at, :]
        s_ref[0, j] = sj
        return y + sj * r_ref[0, j, at, :]

    y = j_loop(update, jnp.zeros_like(v))
    mu = jnp.mean(y, axis=0, keepdims=True)
    yc = y - mu
    var = jnp.mean(yc * yc, axis=0, keepdims=True)
    yn = (yc * lax.rsqrt(var + GN_EPS)) * lnw_ref[...] + lnb_ref[...]
    y_ref[0, t] = yn + bonus_ref[at, :] * v


def _blocked_j_loop(hd):
    def run(body, init):
        def block(jb, carry):
            for jj in range(SCAN_J_BLOCK):
                carry = body(jb * SCAN_J_BLOCK + jj, carry)
            return carry
        return lax.fori_loop(0, hd // SCAN_J_BLOCK, block, init)
    return run


def _unrolled_j_loop(hd):
    def run(body, init):
        carry = init
        for j in range(hd):
            carry = body(j, carry)
        return carry
    return run


def _scan_kernel(*refs, tc):
    seq_refs, par_refs, s0_ref = refs[:5], refs[5:10], refs[10]
    y_ref, s_ref = refs[11:13]
    scr_refs = refs[13:]

    @pl.when(pl.program_id(1) == 0)
    def _():
        s_ref[...] = s0_ref[...]

    _scan_chunk_setup(seq_refs, par_refs, scr_refs)
    j_loop = _blocked_j_loop(s_ref.shape[1])

    def step(t, carry):
        _scan_step(t, seq_refs, par_refs, scr_refs, y_ref, s_ref, j_loop)
        return carry

    lax.fori_loop(0, tc, step, 0)


def _scan_project_kernel(*refs, tcf, n_tiles, n_col_tiles, n_plain):
    x_ref, wa_ref, wb_ref = refs[:3]
    seq_refs, par_refs = refs[3:8], refs[8:13]
    plain_ref, gate_ref, y_ref, s_ref = refs[13:17]
    w_scr, acc_ref, scr_refs = refs[17], refs[18], refs[19:]
    step_id = pl.program_id(0)

    @pl.when(step_id == 0)
    def _():
        s_ref[...] = jnp.zeros_like(s_ref)

    head = wa_ref.shape[0]
    w_scr[:head] = wa_ref[...].astype(BF16)
    w_scr[head:] = wb_ref[:w_scr.shape[0] - head].astype(BF16)

    _scan_chunk_setup(seq_refs, par_refs, scr_refs)
    j_loop = _unrolled_j_loop(s_ref.shape[1])
    slab = x_ref.shape[0] // PROJECT_SLABS
    for t in range(tcf):
        if t % (tcf // PROJECT_SLABS) == 0:
            rows = pl.ds((t // (tcf // PROJECT_SLABS)) * slab, slab)
            acc_ref[rows, :] = lax.dot_general(x_ref[rows, :], w_scr[...], _NT_DIMS,
                                               preferred_element_type=F32)
        _scan_step(t, seq_refs, par_refs, scr_refs, y_ref, s_ref, j_loop)

    col_tile = jnp.minimum(step_id, n_tiles - 1) % n_col_tiles

    @pl.when(col_tile < n_plain)
    def _():
        plain_ref[...] = acc_ref[...]

    @pl.when(col_tile >= n_plain)
    def _():
        gate_ref[...] = jax.nn.sigmoid(acc_ref[...]).astype(gate_ref.dtype)


def _scan(r, w, k, a, v, params, s0):
    g, hd, t, ln = r.shape
    tc = _tile(t, 64, 1)
    by_chan = pl.BlockSpec((1, hd, tc, ln), lambda gi, c: (gi, 0, c, 0))
    by_time = pl.BlockSpec((1, tc, hd, ln), lambda gi, c: (gi, c, 0, 0))
    per_group = params[0].shape[0] > 1
    par = pl.BlockSpec((None, hd, ln), lambda gi, c: (gi if per_group else 0, 0, 0))
    st = pl.BlockSpec((1, hd, hd, ln), lambda gi, c: (gi, 0, 0, 0))
    return pl.pallas_call(
        functools.partial(_scan_kernel, tc=tc),
        grid=(g, t // tc),
        in_specs=[by_chan] * 4 + [by_time] + [par] * 5 + [st],
        out_specs=[by_time, st],
        out_shape=[jax.ShapeDtypeStruct(v.shape, F32), jax.ShapeDtypeStruct(s0.shape, F32)],
        scratch_shapes=[pltpu.VMEM((hd, tc, ln), F32)] * 3 + [pltpu.VMEM((tc, ln), F32)],
        compiler_params=_cp(("arbitrary", "arbitrary")),
        name="rwkv_scan",
    )(r, w, k, a, v, *params, s0)


def _scan_project_fits(m, t, groups, n_rows, tn=512):
    tiles = (m // _tile(m, 1100)) * (n_rows // tn)
    return groups == 1 and n_rows % tn == 0 and t % SCAN_FUSED_STEPS == 0 and t // SCAN_FUSED_STEPS >= tiles


def _scan_project(x, wt, row0, plain_rows, gate_rows, r, w, k, a, v, params, tn=512):
    m, kdim = x.shape
    _, hd, t, ln = r.shape
    tm = _tile(m, 1100)
    n_plain, n_gate = plain_rows // tn, gate_rows // tn
    nct = n_plain + n_gate
    tiles = (m // tm) * nct
    tcf = SCAN_FUSED_STEPS
    assert plain_rows % tn == 0 and gate_rows % tn == 0 and t // tcf >= tiles
    head = tn - row0 % tn
    assert head % 16 == 0 and row0 % head == 0 and tn % head == 0
    ja, jb, stride = row0 // head, (row0 + head) // tn, tn // head
    tile = lambda s: jnp.minimum(s, tiles - 1)
    row_t = lambda s: tile(s) // nct
    col_t = lambda s: tile(s) % nct
    by_chan = pl.BlockSpec((1, hd, tcf, ln), lambda s: (0, 0, s, 0))
    by_time = pl.BlockSpec((1, tcf, hd, ln), lambda s: (0, s, 0, 0))
    par = pl.BlockSpec((None, hd, ln), lambda s: (0, 0, 0))
    st = pl.BlockSpec((1, hd, hd, ln), lambda s: (0, 0, 0, 0))
    return pl.pallas_call(
        functools.partial(_scan_project_kernel, tcf=tcf, n_tiles=tiles, n_col_tiles=nct, n_plain=n_plain),
        grid=(t // tcf,),
        in_specs=[pl.BlockSpec((tm, kdim), lambda s: (row_t(s), 0), pipeline_mode=pl.Buffered(1)),
                  pl.BlockSpec((head, kdim), lambda s: (ja + stride * col_t(s), 0)),
                  pl.BlockSpec((tn, kdim), lambda s: (jb + col_t(s), 0))]
        + [by_chan] * 4 + [by_time] + [par] * 5,
        out_specs=[pl.BlockSpec((tm, tn), lambda s: (row_t(s), jnp.minimum(col_t(s), n_plain - 1))),
                   pl.BlockSpec((tm, tn), lambda s: (row_t(s), jnp.maximum(col_t(s) - n_plain, 0))),
                   by_time, st],
        out_shape=[jax.ShapeDtypeStruct((m, plain_rows), F32), jax.ShapeDtypeStruct((m, gate_rows), BF16),
                   jax.ShapeDtypeStruct(v.shape, F32), jax.ShapeDtypeStruct((1, hd, hd, ln), F32)],
        scratch_shapes=[pltpu.VMEM((tn, kdim), BF16), pltpu.VMEM((tm, tn), F32)]
        + [pltpu.VMEM((hd, tcf, ln), F32)] * 3 + [pltpu.VMEM((tcf, ln), F32)],
        compiler_params=_cp(("arbitrary",)),
        name="rwkv_scan_project",
    )(x, wt, wt, r, w, k, a, v, *params)


class _Lanes:
    def __init__(self, b, n_heads):
        self.h = n_heads
        self.by_batch = b % V7X_LANES == 0
        self.bi = V7X_LANES if self.by_batch else max(1, min(b, V7X_LANES // n_heads))
        self.bo = b // self.bi
        self.groups = self.bo * n_heads if self.by_batch else self.bo
        self.lanes = self.bi if self.by_batch else self.bi * n_heads

    def rows_by_time(self, x2d, t):
        x = x2d.reshape(self.bo, self.bi, t, self.h, HEAD_A)
        x = x.transpose(0, 3, 2, 4, 1) if self.by_batch else x.transpose(0, 2, 4, 1, 3)
        return x.reshape(self.groups, t, HEAD_A, self.lanes)

    def rows_by_chan(self, x2d, t):
        x = x2d.reshape(self.bo, self.bi, t, HEAD_A, self.h)
        x = x.transpose(0, 4, 3, 2, 1) if self.by_batch else x.transpose(0, 3, 2, 1, 4)
        return x.reshape(self.groups, HEAD_A, t, self.lanes)

    def rows_back(self, y, t):
        if self.by_batch:
            y = y.reshape(self.bo, self.h, t, HEAD_A, self.bi).transpose(0, 4, 2, 1, 3)
        else:
            y = y.reshape(self.bo, t, HEAD_A, self.bi, self.h).transpose(0, 3, 1, 4, 2)
        return y.reshape(self.bo * self.bi * t, self.h * HEAD_A)

    def param(self, p):
        p = p.reshape(self.h, HEAD_A)
        if self.by_batch:
            p = jnp.broadcast_to(p[None, :, :, None], (self.bo, self.h, HEAD_A, self.bi))
            return p.reshape(self.groups, HEAD_A, self.lanes)
        return jnp.tile(p.T, (1, self.bi))[None]

    def state_in(self, s):
        s = s.astype(F32).reshape(self.bo, self.bi, self.h, HEAD_A, HEAD_A)
        s = s.transpose(0, 2, 4, 3, 1) if self.by_batch else s.transpose(0, 4, 3, 1, 2)
        return s.reshape(self.groups, HEAD_A, HEAD_A, self.lanes)

    def state_out(self, s):
        if self.by_batch:
            s = s.reshape(self.bo, self.h, HEAD_A, HEAD_A, self.bi).transpose(0, 4, 1, 3, 2)
        else:
            s = s.reshape(self.bo, HEAD_A, HEAD_A, self.bi, self.h).transpose(0, 3, 4, 2, 1)
        return s.reshape(self.bo * self.bi, self.h, HEAD_A, HEAD_A)


def _rwkv_group(lay, rwka, v, t, s_prev, scan_params):
    params = [lay.param(p) for p in scan_params]
    if s_prev is None:
        s0 = jnp.zeros((lay.groups, HEAD_A, HEAD_A, lay.lanes), F32)
    else:
        s0 = lay.state_in(s_prev)
    y, s_new = _scan(*rwka, lay.rows_by_time(v, t), params, s0)
    return lay.rows_back(y, t), lay.state_out(s_new)


def _gate_kernel(*refs):
    y_ref, g_ref, o_ref = refs[-3:]
    o_ref[...] = (y_ref[...] * g_ref[...]).astype(o_ref.dtype)


def _gate_rows(y, g, row0, prev_out):
    rows, d_a = y.shape
    tr = _tile(_gcd(row0, rows) if row0 else rows, 512, 16)
    off = row0 // tr
    row = pl.BlockSpec((tr, d_a), lambda i: (i, 0))
    return pl.pallas_call(
        _gate_kernel,
        grid=(rows // tr,),
        in_specs=[pl.BlockSpec(memory_space=pl.ANY), row, row],
        out_specs=pl.BlockSpec((tr, d_a), lambda i: (off + i, 0)),
        out_shape=jax.ShapeDtypeStruct(prev_out.shape, BF16),
        input_output_aliases={0: 0},
        compiler_params=_cp(("arbitrary",)),
        name="rwkv_gate",
    )(prev_out, y, g)


def _conv_kernel(*refs, tr, seq_len, mode):
    gb_ref, gc_ref, u_ref, aux1_ref, aux2_ref, cw_ref = refs[-8:-2]
    y_ref, u2_ref = refs[-2:]
    f1_ref, f2_ref = aux1_ref, aux2_ref
    u2 = gc_ref[...] * u_ref[...]
    if mode == "halo":
        halo = aux1_ref[...] * aux2_ref[...]
        row0 = pl.program_id(0) * tr
        p1 = _shift_rows(u2, 1, row0, seq_len, halo)
        p2 = _shift_rows(u2, 2, row0, seq_len, halo)
    else:
        t = lax.broadcasted_iota(jnp.int32, u2.shape, 0) % seq_len
        p1 = jnp.where(t < 1, f1_ref[...], pltpu.roll(u2, 1, axis=0))
        p2 = jnp.where(t < 2, f2_ref[...], pltpu.roll(u2, 2, axis=0))
    z = p2 * cw_ref[0:1, :] + p1 * cw_ref[1:2, :]
    z = z + u2 * cw_ref[2:3, :]
    y_ref[...] = (gb_ref[...] * z).astype(y_ref.dtype)
    u2_ref[...] = u2


def _conv(p_bc, row0, rows, seq_len, firsts, conv_w, prev_out):
    n = p_bc.shape[0]
    d_b = conv_w.shape[1]
    mode = "halo" if firsts is None else "first"
    if mode == "halo":
        tr = _tile(_gcd(seq_len, rows), 256, 16)
    else:
        tr = _tile(rows, 256, _lcm(seq_len, 16))
    assert row0 % tr == 0 and rows % tr == 0 and seq_len >= CONV_W - 1
    off = row0 // tr
    col = lambda c: pl.BlockSpec((tr, d_b), lambda i: (off + i, c))
    in_specs = [col(0), col(1), col(2)]
    args = [p_bc, p_bc, p_bc]
    if mode == "halo":
        hb = tr // V7X_SUBLANES
        hcol = lambda c: pl.BlockSpec(
            (V7X_SUBLANES, d_b), lambda i: (jnp.maximum((off + i) * hb - 1, 0), c))
        in_specs += [hcol(1), hcol(2)]
        args += [p_bc, p_bc]
    else:
        in_specs += [pl.BlockSpec((tr, d_b), lambda i: (i, 0))] * 2
        args += list(firsts)
    in_specs.append(pl.BlockSpec(conv_w.shape, lambda i: (0, 0)))
    args.append(conv_w)
    in_specs = [pl.BlockSpec(memory_space=pl.ANY)] * 2 + in_specs
    args = list(prev_out) + args
    out_spec = pl.BlockSpec((tr, d_b), lambda i: (off + i, 0))
    return pl.pallas_call(
        functools.partial(_conv_kernel, tr=tr, seq_len=seq_len, mode=mode),
        grid=(rows // tr,),
        in_specs=in_specs,
        out_specs=[out_spec, out_spec],
        out_shape=[jax.ShapeDtypeStruct((n, d_b), BF16), jax.ShapeDtypeStruct((n, d_b), F32)],
        input_output_aliases={0: 0, 1: 1},
        compiler_params=_cp(("arbitrary",)),
        name="conv_" + mode,
    )(*args)


def _attn_kernel(*refs, bb, tq, heads):
    q_ref, k_ref, v_ref, o_ref = refs[-4:]
    dh = q_ref.shape[1] // heads
    scale = dh ** -0.5
    rows = bb * tq
    row_batch = lax.broadcasted_iota(jnp.int32, (rows, dh), 0) // tq
    cols = [slice(h * dh, (h + 1) * dh) for h in range(heads)]
    qs = [q_ref[:, c].astype(BF16) for c in cols]
    pairs = [(b, h) for b in range(bb) for h in range(heads)]
    s = jnp.concatenate(
        [lax.dot_general(qs[h], k_ref[b, :, cols[h]].astype(BF16), _NT_DIMS, preferred_element_type=F32)
         for b, h in pairs], axis=0) * scale
    p = jnp.exp(s - jnp.max(s, axis=-1, keepdims=True))
    p = (p / jnp.sum(p, axis=-1, keepdims=True)).astype(BF16)
    outs = [None] * heads
    for n, (b, h) in enumerate(pairs):
        o = jnp.dot(p[n * rows:(n + 1) * rows], v_ref[b, :, cols[h]].astype(BF16), preferred_element_type=F32)
        outs[h] = o if outs[h] is None else jnp.where(row_batch == b, o, outs[h])
    for h in range(heads):
        o_ref[:, cols[h]] = outs[h].astype(o_ref.dtype)


def _attention(q_src, q_col, row0, b, t, mem_k, mem_v, heads, prev_out):
    n_mem, d_c = mem_k.shape[1:]
    if t >= 128:
        bb, tq = 1, _tile(t, 512, 16)
    else:
        kv_block_cap = max(1, ATTN_KV_BLOCK_BYTES // (n_mem * d_c * mem_k.dtype.itemsize))
        bb, tq = _tile(b, min(kv_block_cap, max(1, 64 // t)), 1), t
    assert (bb * tq) % 16 == 0 and row0 % (bb * tq) == 0
    tpb = t // tq
    off = row0 // (bb * tq)
    qmap = lambda i, j: (off + i * tpb + j, q_col)
    omap = lambda i, j: (off + i * tpb + j, 0)
    kv = pl.BlockSpec((bb, n_mem, d_c), lambda i, j: (i, 0, 0))
    return pl.pallas_call(
        functools.partial(_attn_kernel, bb=bb, tq=tq, heads=heads),
        grid=(b // bb, tpb),
        in_specs=[pl.BlockSpec(memory_space=pl.ANY), pl.BlockSpec((bb * tq, d_c), qmap), kv, kv],
        out_specs=pl.BlockSpec((bb * tq, d_c), omap),
        out_shape=jax.ShapeDtypeStruct(prev_out.shape, BF16),
        input_output_aliases={0: 0},
        compiler_params=_cp(("arbitrary", "arbitrary")),
        name="mem_attention",
    )(prev_out, q_src, mem_k, mem_v)


def _merge_kernel(ya_ref, yb_ref, yc_ref, ga_ref, gb_ref, gc_ref, wa_ref, wb_ref, wc_ref, o_ref):
    def branch(y_ref, g_ref, w_ref):
        return g_ref[...].astype(F32) * jnp.dot(
            y_ref[...], w_ref[...].astype(BF16), preferred_element_type=F32)

    acc = branch(ya_ref, ga_ref, wa_ref) + branch(yb_ref, gb_ref, wb_ref)
    o_ref[...] = (acc + branch(yc_ref, gc_ref, wc_ref)).astype(o_ref.dtype)


def _merge(ya, yb, yc, gates, wa, wb, wc):
    n, d = ya.shape[0], wa.shape[1]
    tm = _tile(n, 1100)
    tn = _tile(d, 512, V7X_LANES)
    nj = d // tn
    act = lambda y: pl.BlockSpec((tm, y.shape[1]), lambda i, j: (i, 0))
    gate = lambda c: pl.BlockSpec((tm, tn), lambda i, j: (i, c * nj + j))
    wsp = lambda w: pl.BlockSpec((w.shape[0], tn), lambda i, j: (0, j))
    return pl.pallas_call(
        _merge_kernel,
        grid=(n // tm, nj),
        in_specs=[act(ya), act(yb), act(yc), gate(0), gate(1), gate(2), wsp(wa), wsp(wb), wsp(wc)],
        out_specs=pl.BlockSpec((tm, tn), lambda i, j: (i, j)),
        out_shape=jax.ShapeDtypeStruct((n, d), BF16),
        compiler_params=_cp(("arbitrary", "arbitrary")),
        name="branch_merge",
    )(ya, yb, yc, gates, gates, gates, wa, wb, wc)


def _swap_head_chan(p, n_heads, to_chan_major, axis=-1):
    axis = axis % p.ndim
    blocks = p.shape[axis] // (n_heads * HEAD_A)
    inner = (n_heads, HEAD_A) if to_chan_major else (HEAD_A, n_heads)
    q = p.reshape(p.shape[:axis] + (blocks,) + inner + p.shape[axis + 1:])
    return jnp.swapaxes(q, axis + 1, axis + 2).reshape(p.shape)


def _pad_rows(w, row0, total):
    return jnp.zeros((total, w.shape[1]), BF16).at[row0:row0 + w.shape[0]].set(w.astype(BF16))


def _first_rows(state, seq_len, width):
    b, k, c = state.shape
    out = jnp.zeros((b, seq_len, width), F32).at[:, :k, :c].set(state.astype(F32))
    return out.reshape(b * seq_len, width)


def _layer(x, h1, groups, lp, mem_kv, heads_c):
    n, d = x.shape
    d_a = lp["w0"].shape[0]
    d_b = lp["conv_w"].shape[1]
    p_a_w = lp["mu_shift"].shape[0]
    lora_w, lora_a, lora_g = (lp[q].shape[0] for q in ("w_lora_up", "a_lora_up", "g_lora_up"))
    d_c = lp["w_br_c"].shape[0]
    assert d_c == d_b and p_a_w == 3 * d_a + lora_w + lora_a + lora_g

    x = _matmul(_gateup(h1, lp["w_ffn1_in"]), lp["w_ffn1_out"], residual=x, tn=256, x_resident=True)
    h = _norm(x, lp["g_mix"])

    w_t = jnp.swapaxes(lp["w_in"], 0, 1)
    pw = _round_up(p_a_w, 512)
    w_a = jnp.concatenate([_swap_head_chan(w_t[:2 * d_a], d_a // HEAD_A, to_chan_major=True, axis=0),
                           w_t[2 * d_a:pw]], axis=0)
    p_a = _matmul_nt(h, w_a)
    p_bc = gates = None

    pw = p_a.shape[1]
    low_w = pw - 3 * d_a
    n_heads = d_a // HEAD_A
    rk_swap = lambda p: jnp.concatenate(
        [_swap_head_chan(p[..., :2 * d_a], n_heads, to_chan_major=True), p[..., 2 * d_a:]], axis=-1)
    to_cm = lambda p: _swap_head_chan(p, n_heads, to_chan_major=True)
    pre_params = (
        jnp.pad(rk_swap(lp["mu_shift"]), (0, pw - p_a_w)).reshape(1, pw),
        to_cm(lp["w0"]).reshape(1, d_a), to_cm(lp["a0"]).reshape(1, d_a),
        _pad_rows(to_cm(lp["w_lora_up"]), 0, low_w),
        _pad_rows(to_cm(lp["a_lora_up"]), lora_w, low_w),
        _pad_rows(lp["g_lora_up"], lora_w + lora_a, low_w),
    )
    scan_params = (lp["k_k"], lp["k_a"], lp["r_k"].reshape(-1), lp["lnx_w"], lp["lnx_b"])
    ya, new_states = jnp.zeros((n, d_a), BF16), []
    for row0, b, t, st in groups:
        lay = _Lanes(b, n_heads)
        if st is None:
            assert not lay.by_batch
            *rwka, v, g, last = _rwkv_pre_lanes(p_a, row0, b, t, pre_params, p_a_w - 3 * d_a, lay.bi)
            last = last[:, :p_a_w]
        else:
            first = _first_rows(rk_swap(st["shift"])[:, None, :], t, pw)
            r, w, k, v, a, g = _rwkv_pre_first(p_a, row0, b * t, t, first, pre_params, p_a_w - 3 * d_a)
            rwka = [lay.rows_by_chan(q, t) for q in (r, w, k, a)]
            last = p_a[row0 + t - 1:row0 + b * t:t, :p_a_w]
        rest_rows = (3 * d_b + d_c, 3 * d)
        if p_bc is None and st is None and _scan_project_fits(n, t, lay.groups, sum(rest_rows)):
            p_bc, gates, y, s_new = _scan_project(
                h, w_t, p_a_w, *rest_rows, *rwka, lay.rows_by_time(v, t),
                [lay.param(p) for p in scan_params])
            y, s_new = lay.rows_back(y, t), lay.state_out(s_new)
        else:
            y, s_new = _rwkv_group(lay, rwka, v, t, None if st is None else st["rwkv"], scan_params)
        ya = _gate_rows(y, g, row0, ya)
        shift_new = jnp.concatenate(
            [_swap_head_chan(last[:, :2 * d_a], n_heads, to_chan_major=False), last[:, 2 * d_a:]], axis=1)
        new_states.append({"rwkv": s_new, "shift": shift_new})

    if p_bc is None:
        p_bc = _matmul_nt(h, w_t, row0=p_a_w, rows=3 * d_b + d_c)
        gates = _matmul_nt(h, w_t, row0=p_a_w + 3 * d_b + d_c, rows=3 * d, out_dtype=BF16, act="sigmoid")

    conv = (jnp.zeros((n, d_b), BF16), jnp.zeros((n, d_b), F32))
    for row0, b, t, st in groups:
        firsts = None
        if st is not None:
            buf = st["conv"]
            firsts = (_first_rows(buf[:, 1:, :], t, d_b), _first_rows(buf, t, d_b))
        conv = _conv(p_bc, row0, b * t, t, firsts, lp["conv_w"], conv)
    yb, u2 = conv
    for (row0, b, t, st), ns in zip(groups, new_states):
        tail = [u2[row0 + t - q:row0 + b * t:t] for q in range(CONV_W - 1, 0, -1)]
        ns["conv"] = jnp.stack(tail, axis=1)

    yc = jnp.zeros((n, d_c), BF16)
    for (row0, b, t, st), (mk, mv) in zip(groups, mem_kv):
        yc = _attention(p_bc, (3 * d_b) // d_c, row0, b, t,
                        mk.reshape(b, -1, d_c), mv.reshape(b, -1, d_c), heads_c, yc)

    merged = _merge(ya, yb, yc, gates, lp["w_br_a"], lp["w_br_b"], lp["w_br_c"])
    x = _matmul(merged, lp["w_out"], residual=x)

    h = _norm(x, lp["g_ffn2"])
    x = _matmul(_gateup(h, lp["w_ffn2_in"]), lp["w_ffn2_out"], residual=x, tn=256, x_resident=True)
    return x, new_states


def kernel(x_prompt, x_sample, mem_prompt, cache_mem_k, cache_mem_v, state_rwkv, state_shift, state_conv, g_ffn1, w_ffn1_in, w_ffn1_out, g_mix, w_in, mu_shift, w0, w_lora_up, a0, a_lora_up, g_lora_up, k_k, k_a, r_k, lnx_w, lnx_b, conv_w, g_mem, w_mem_kv, w_br_a, w_br_b, w_br_c, w_out, g_ffn2, w_ffn2_in, w_ffn2_out, g_final):
    bp, tp, d = x_prompt.shape
    bs, ts, _ = x_sample.shape
    n_p, n_s = bp * tp, bs * ts
    depth = w_in.shape[0]
    n_mem, heads_c, dh_c = cache_mem_k.shape[2:]
    d_c = heads_c * dh_c

    per_layer = dict(
        g_ffn1=g_ffn1, w_ffn1_in=w_ffn1_in, w_ffn1_out=w_ffn1_out, g_mix=g_mix, w_in=w_in,
        mu_shift=mu_shift, w0=w0, w_lora_up=w_lora_up, a0=a0, a_lora_up=a_lora_up,
        g_lora_up=g_lora_up, k_k=k_k, k_a=k_a, r_k=r_k, lnx_w=lnx_w, lnx_b=lnx_b, conv_w=conv_w,
        w_br_a=w_br_a, w_br_b=w_br_b, w_br_c=w_br_c, w_out=w_out, g_ffn2=g_ffn2,
        w_ffn2_in=w_ffn2_in, w_ffn2_out=w_ffn2_out)

    mem_rows = mem_prompt.reshape(bp * n_mem, d)
    outs = {q: [] for q in ("mk", "mv", "rp", "shp", "cp", "rs", "shs", "cs")}
    x = None
    for l in range(depth):
        lp = {q: p[l] for q, p in per_layer.items()}
        if l == 0:
            x, h1 = _norm_concat(x_prompt.reshape(n_p, d), x_sample.reshape(n_s, d), lp["g_ffn1"])
        else:
            h1 = _norm(x, lp["g_ffn1"])
        kv = _matmul(_norm(mem_rows, g_mem[l]), w_mem_kv[l])
        mk_p = kv[:, :d_c].reshape(bp, n_mem, heads_c, dh_c)
        mv_p = kv[:, d_c:2 * d_c].reshape(bp, n_mem, heads_c, dh_c)
        groups = [
            (0, bp, tp, None),
            (n_p, bs, ts, {"rwkv": state_rwkv[l], "shift": state_shift[l], "conv": state_conv[l]}),
        ]
        prompt_kv = (kv[:, :d_c].reshape(bp, n_mem, d_c), kv[:, d_c:2 * d_c].reshape(bp, n_mem, d_c))
        x, (st_p, st_s) = _layer(
            x, h1, groups, lp, [prompt_kv, (cache_mem_k[l], cache_mem_v[l])], heads_c)
        outs["mk"].append(mk_p)
        outs["mv"].append(mv_p)
        for tag, st in (("p", st_p), ("s", st_s)):
            outs["r" + tag].append(st["rwkv"])
            outs["sh" + tag].append(st["shift"])
            outs["c" + tag].append(st["conv"])
    y_prompt = _norm(x, g_final, F32, 0, n_p).reshape(bp, tp, d)
    y_sample = _norm(x, g_final, F32, n_p, n_s).reshape(bs, ts, d)
    stack = lambda q: jnp.stack(outs[q], 0)
    return (y_prompt, y_sample, stack("mk"), stack("mv"), stack("rp"), stack("shp"), stack("cp"),
            stack("rs"), stack("shs"), stack("cs"))
```
